```python
import math
import jax
import jax.numpy as jnp
from jax import lax
import numpy as np

D_MODEL = 1024
BATCH = 2
SEQ = 8192
DEPTH = 4
DEC_BATCH = 32
DEC_SEQ = 8
PAST_LEN = 8192
PAGE_SIZE = 128

HEAD_DIM = 64
A_HEADS = 4
A_VDIM = 2 * HEAD_DIM
B_HEADS = 8
MOBA_BLOCK = 256
MOBA_TOPK = 3
C_HEADS = 16
C_KV_HEADS = 4
IDX_HEADS = 8
IDX_DIM = 64
DSA_TOPK = 256
D_FF = 2816
CONV_W = 3
ROPE_THETA = 10000.0
EPS = 1e-6
Q_BLOCK = 128
MOBA_Q_BLOCK = 32
N_EVEN = (DEPTH + 1) // 2
N_ODD = DEPTH // 2

A_QK = A_HEADS * 2 * HEAD_DIM
A_V = A_HEADS * A_VDIM
B_W = B_HEADS * HEAD_DIM
EVEN_SPLITS = (A_QK, A_QK, A_V, B_W, B_W, B_W)
EVEN_K_W = A_QK + B_W
EVEN_V_W = A_V + B_W
C_Q = C_HEADS * HEAD_DIM
C_KV = C_KV_HEADS * HEAD_DIM
ODD_SPLITS = (C_Q, C_KV, C_KV, IDX_HEADS * IDX_DIM, IDX_DIM, IDX_HEADS)
IDX_W_SCALE = (IDX_HEADS * IDX_DIM) ** -0.5
ATTN_SCALE = HEAD_DIM ** -0.5

kernel_name = 'hybrid_diff_moba_dsa_convffn_step'


def split_cols(a, sizes):
    return jnp.split(a, [int(s) for s in np.cumsum(sizes)[:-1]], axis=-1)


def rmsnorm(x, g):
    xf = x.astype(jnp.float32)
    y = xf * lax.rsqrt(jnp.mean(xf * xf, axis=-1, keepdims=True) + EPS)
    return (y * g.astype(jnp.float32)).astype(x.dtype)


def rope(x, pos):
    d = x.shape[-1]
    half = d // 2
    inv = ROPE_THETA ** (-jnp.arange(half, dtype=jnp.float32) * 2.0 / d)
    ang = pos.astype(jnp.float32)[:, None] * inv[None, :]
    cos = jnp.cos(ang)[None, :, None, :]
    sin = jnp.sin(ang)[None, :, None, :]
    xf = x.astype(jnp.float32)
    x1, x2 = xf[..., :half], xf[..., half:]
    return jnp.concatenate([x1 * cos - x2 * sin, x2 * cos + x1 * sin], axis=-1).astype(x.dtype)


def map_query_blocks(fn, qs, q_pos, block):
    B, Lq = qs[0].shape[:2]
    nb = Lq // block

    def split(a):
        return jnp.swapaxes(a.reshape((B, nb, block) + a.shape[2:]), 0, 1)

    out = lax.map(lambda args: fn(*args[0], args[1]), ([split(q) for q in qs], q_pos.reshape(nb, block)))
    out = jnp.swapaxes(out, 0, 1)
    return out.reshape((B, Lq) + out.shape[3:])


def diff_attn_core(q1, q2, k1, k2, v, q_pos, k_pos, lam):
    mask = k_pos[None, :] <= q_pos[:, None]

    def probs(q, k):
        s = jnp.einsum('bqhd,blhd->bhql', q, k).astype(jnp.float32) * ATTN_SCALE
        return jax.nn.softmax(jnp.where(mask, s, -jnp.inf), axis=-1)

    p = probs(q1, k1) - lam * probs(q2, k2)
    return jnp.einsum('bhql,blhe->bqhe', p.astype(v.dtype), v)


def moba_blocks(k, v):
    B, L, H, d = k.shape
    nbp = max(-(-L // MOBA_BLOCK), MOBA_TOPK)
    pad = nbp * MOBA_BLOCK - L

    def blocks(a):
        a = jnp.pad(a, ((0, 0), (0, pad), (0, 0), (0, 0)))
        return a.reshape(B, nbp, MOBA_BLOCK, H, d).transpose(0, 3, 1, 2, 4)

    kb, vb = blocks(k), blocks(v)
    kmean = jnp.mean(kb.astype(jnp.float32), axis=3).astype(k.dtype)
    return kb, vb, kmean


def moba_core(q, q_pos, kblk, vblk, kmean):
    Q = q_pos.shape[0]
    nbp = kmean.shape[2]
    own = q_pos // MOBA_BLOCK
    gate = jnp.einsum('bqhd,bhnd->bhqn', q, kmean).astype(jnp.float32)
    fully_past = jnp.arange(nbp)[None, :] < own[:, None]
    gate = jnp.where(fully_past, gate, -jnp.inf)
    _, top = lax.top_k(gate, MOBA_TOPK)
    own_b = jnp.broadcast_to(own[None, None, :, None], top.shape[:3] + (1,)).astype(top.dtype)
    sel = jnp.concatenate([top, own_b], axis=-1)
    take = jax.vmap(jax.vmap(lambda blocks, idx: blocks[idx]))
    kg = take(kblk, sel)
    vg = take(vblk, sel)
    s = jnp.einsum('bqhd,bhqnjd->bhqnj', q, kg).astype(jnp.float32) * ATTN_SCALE
    top_ok = jnp.arange(MOBA_TOPK)[None, :] < own[:, None]
    own_ok = own[:, None] * MOBA_BLOCK + jnp.arange(MOBA_BLOCK)[None, :] <= q_pos[:, None]
    ok = jnp.concatenate([jnp.broadcast_to(top_ok[:, :, None], (Q, MOBA_TOPK, MOBA_BLOCK)), own_ok[:, None, :]], axis=1)
    s = jnp.where(ok, s, -jnp.inf)
    B, H = s.shape[:2]
    p = jax.nn.softmax(s.reshape(B, H, Q, -1), axis=-1).reshape(s.shape).astype(vg.dtype)
    return jnp.einsum('bhqnj,bhqnjd->bqhd', p, vg)


def dsa_core(q, qi, wi, q_pos, k, v, ki, topk):
    B, Q = q.shape[:2]
    L = k.shape[1]
    logits = jnp.einsum('bqhe,ble->bqhl', qi, ki).astype(jnp.float32)
    score = jnp.einsum('bqh,bqhl->bql', wi.astype(jnp.float32), jax.nn.relu(logits))
    k_pos = jnp.arange(L, dtype=jnp.int32)
    score = jnp.where(k_pos[None, None, :] <= q_pos[None, :, None], score, -jnp.inf)
    _, sel = lax.top_k(score, topk)
    ok = sel <= q_pos[None, :, None]
    take = jax.vmap(lambda a, s: a[s])
    kg = take(k, sel)
    vg = take(v, sel)
    qg = q.reshape(B, Q, C_KV_HEADS, C_HEADS // C_KV_HEADS, HEAD_DIM)
    s = jnp.einsum('bqkgd,bqnkd->bqkgn', qg, kg).astype(jnp.float32) * ATTN_SCALE
    s = jnp.where(ok[:, :, None, None, :], s, -jnp.inf)
    p = jax.nn.softmax(s, axis=-1).astype(vg.dtype)
    o = jnp.einsum('bqkgn,bqnkd->bqkgd', p, vg)
    return o.reshape(B, Q, C_HEADS, HEAD_DIM)


def even_mixer(h, pos, k_past, v_past, w_in, w_out, lam_p, subln_g, layer_idx):
    B, Lq, _ = h.shape
    qa, ka, va, qb, kb, vb = split_cols(h @ w_in, EVEN_SPLITS)
    qa = rope(qa.reshape(B, Lq, 2 * A_HEADS, HEAD_DIM), pos).reshape(B, Lq, A_HEADS, 2, HEAD_DIM)
    ka = rope(ka.reshape(B, Lq, 2 * A_HEADS, HEAD_DIM), pos).reshape(B, Lq, A_QK)
    qb = rope(qb.reshape(B, Lq, B_HEADS, HEAD_DIM), pos)
    kb = rope(kb.reshape(B, Lq, B_HEADS, HEAD_DIM), pos).reshape(B, Lq, B_W)
    k_new = jnp.concatenate([ka, kb], axis=-1)
    v_new = jnp.concatenate([va, vb], axis=-1)
    if k_past is None:
        k_all, v_all = k_new, v_new
    else:
        k_all = jnp.concatenate([k_past.astype(k_new.dtype), k_new], axis=1)
        v_all = jnp.concatenate([v_past.astype(v_new.dtype), v_new], axis=1)
    L = k_all.shape[1]
    k_pos = jnp.arange(L, dtype=jnp.int32)
    ka_all = k_all[..., :A_QK].reshape(B, L, A_HEADS, 2, HEAD_DIM)
    va_all = v_all[..., :A_V].reshape(B, L, A_HEADS, A_VDIM)
    kb_all = k_all[..., A_QK:].reshape(B, L, B_HEADS, HEAD_DIM)
    vb_all = v_all[..., A_V:].reshape(B, L, B_HEADS, HEAD_DIM)
    lam_init = 0.8 - 0.6 * math.exp(-0.3 * layer_idx)
    lp = lam_p.astype(jnp.float32)
    lam = jnp.exp(jnp.sum(lp[0] * lp[1])) - jnp.exp(jnp.sum(lp[2] * lp[3])) + lam_init
    k1, k2 = ka_all[..., 0, :], ka_all[..., 1, :]
    blk = Q_BLOCK if Lq % Q_BLOCK == 0 else Lq
    oa = map_query_blocks(lambda q1, q2, p: diff_attn_core(q1, q2, k1, k2, va_all, p, k_pos, lam),
                          [qa[..., 0, :], qa[..., 1, :]], pos, blk)
    oa = rmsnorm(oa, subln_g) * (1.0 - lam_init)
    kblk, vblk, kmean = moba_blocks(kb_all, vb_all)
    mblk = MOBA_Q_BLOCK if Lq % MOBA_Q_BLOCK == 0 else Lq
    ob = map_query_blocks(lambda q, p: moba_core(q, p, kblk, vblk, kmean), [qb], pos, mblk)
    o = jnp.concatenate([oa.reshape(B, Lq, A_V), ob.reshape(B, Lq, B_W)], axis=-1)
    return o @ w_out, k_new, v_new


def odd_mixer(h, pos, k_past, v_past, ki_past, w_in, w_out):
    B, Lq, _ = h.shape
    q, k, v, qi, ki, wi = split_cols(h @ w_in, ODD_SPLITS)
    q = rope(q.reshape(B, Lq, C_HEADS, HEAD_DIM), pos)
    k = rope(k.reshape(B, Lq, C_KV_HEADS, HEAD_DIM), pos).reshape(B, Lq, C_KV)
    qi = rope(qi.reshape(B, Lq, IDX_HEADS, IDX_DIM), pos)
    ki = rope(ki[:, :, None, :], pos)[:, :, 0, :]
    wi = wi * IDX_W_SCALE
    if k_past is None:
        k_all, v_all, ki_all = k, v, ki
    else:
        k_all = jnp.concatenate([k_past.astype(k.dtype), k], axis=1)
        v_all = jnp.concatenate([v_past.astype(v.dtype), v], axis=1)
        ki_all = jnp.concatenate([ki_past.astype(ki.dtype), ki], axis=1)
    L = k_all.shape[1]
    topk = min(DSA_TOPK, L // 4)
    kh = k_all.reshape(B, L, C_KV_HEADS, HEAD_DIM)
    vh = v_all.reshape(B, L, C_KV_HEADS, HEAD_DIM)
    blk = Q_BLOCK if Lq % Q_BLOCK == 0 else Lq
    o = map_query_blocks(lambda qq, qqi, ww, p: dsa_core(qq, qqi, ww, p, kh, vh, ki_all, topk),
                         [q, qi, wi], pos, blk)
    return o.reshape(B, Lq, C_Q) @ w_out, k, v, ki


def conv_ffn(h, buf, w_g, w_u, cw, cb, w_d):
    L = h.shape[1]
    g = h @ w_g
    gp = jnp.concatenate([buf.astype(g.dtype), g], axis=1)
    c = cb + gp[:, 0:L] * cw[0]
    for j in range(1, CONV_W):
        c = c + gp[:, j:j + L] * cw[j]
    a = jax.nn.silu(c) * (h @ w_u)
    return a @ w_d, gp[:, -(CONV_W - 1):]


def setup_inputs(seed: int = 0) -> dict:
    key = jax.random.key(seed)
    keys = list(jax.random.split(key, 32))
    f32 = jnp.float32

    def nrm(shape, scale):
        return jax.random.normal(keys.pop(), shape, f32) * scale

    n_pages = PAST_LEN // PAGE_SIZE
    n_pool = (5 * DEC_BATCH * n_pages + 3) // 4
    page_table = jax.random.permutation(keys.pop(), n_pool)[: DEC_BATCH * n_pages]
    page_table = page_table.reshape(DEC_BATCH, n_pages).astype(jnp.int32)
    even_in = sum(EVEN_SPLITS)
    odd_in = sum(ODD_SPLITS)
    return {
        'x_prompt': jax.random.normal(keys.pop(), (BATCH, SEQ, D_MODEL), f32),
        'x_sample': jax.random.normal(keys.pop(), (DEC_BATCH, DEC_SEQ, D_MODEL), f32),
        'cache_k_even': jax.random.normal(keys.pop(), (n_pool, N_EVEN, PAGE_SIZE, EVEN_K_W), f32),
        'cache_v_even': jax.random.normal(keys.pop(), (n_pool, N_EVEN, PAGE_SIZE, EVEN_V_W), f32),
        'cache_k_odd': jax.random.normal(keys.pop(), (n_pool, N_ODD, PAGE_SIZE, C_KV), f32),
        'cache_v_odd': jax.random.normal(keys.pop(), (n_pool, N_ODD, PAGE_SIZE, C_KV), f32),
        'cache_kidx_odd': jax.random.normal(keys.pop(), (n_pool, N_ODD, PAGE_SIZE, IDX_DIM), f32),
        'state_conv': jax.random.normal(keys.pop(), (DEPTH, DEC_BATCH, CONV_W - 1, D_FF), f32),
        'page_table': page_table,
        'g_mix': 1.0 + nrm((DEPTH, D_MODEL), 0.02),
        'g_ffn': 1.0 + nrm((DEPTH, D_MODEL), 0.02),
        'g_final': 1.0 + nrm((D_MODEL,), 0.02),
        'w_in_even': nrm((N_EVEN, D_MODEL, even_in), D_MODEL ** -0.5),
        'w_out_even': nrm((N_EVEN, A_V + B_W, D_MODEL), (A_V + B_W) ** -0.5),
        'lam_even': nrm((N_EVEN, 4, HEAD_DIM), 0.1),
        'subln_even': 1.0 + nrm((N_EVEN, A_VDIM), 0.02),
        'w_in_odd': nrm((N_ODD, D_MODEL, odd_in), D_MODEL ** -0.5),
        'w_out_odd': nrm((N_ODD, C_Q, D_MODEL), C_Q ** -0.5),
        'w_gate': nrm((DEPTH, D_MODEL, D_FF), D_MODEL ** -0.5),
        'w_up': nrm((DEPTH, D_MODEL, D_FF), D_MODEL ** -0.5),
        'conv_w': nrm((DEPTH, CONV_W, D_FF), CONV_W ** -0.5),
        'conv_b': nrm((DEPTH, D_FF), 0.02),
        'w_down': nrm((DEPTH, D_FF, D_MODEL), D_FF ** -0.5),
    }


def reference(x_prompt, x_sample, cache_k_even, cache_v_even, cache_k_odd, cache_v_odd, cache_kidx_odd,
              state_conv, page_table, g_mix, g_ffn, g_final, w_in_even, w_out_even, lam_even, subln_even,
              w_in_odd, w_out_odd, w_gate, w_up, conv_w, conv_b, w_down):
    past_len = page_table.shape[1] * PAGE_SIZE

    def paged_rows(cache, i):
        g = cache[page_table, i]
        return g.reshape(g.shape[0], past_len, g.shape[-1])

    def trunk(x, pos, from_cache):
        h = x
        ke, ve, ko, vo, kio, cs = [], [], [], [], [], []
        for l in range(DEPTH):
            e = l // 2
            hn = rmsnorm(h, g_mix[l])
            if l % 2 == 0:
                kp = paged_rows(cache_k_even, e) if from_cache else None
                vp = paged_rows(cache_v_even, e) if from_cache else None
                o, kn, vn = even_mixer(hn, pos, kp, vp, w_in_even[e], w_out_even[e], lam_even[e], subln_even[e], l)
                ke.append(kn)
                ve.append(vn)
            else:
                kp = paged_rows(cache_k_odd, e) if from_cache else None
                vp = paged_rows(cache_v_odd, e) if from_cache else None
                kip = paged_rows(cache_kidx_odd, e) if from_cache else None
                o, kn, vn, kin = odd_mixer(hn, pos, kp, vp, kip, w_in_odd[e], w_out_odd[e])
                ko.append(kn)
                vo.append(vn)
                kio.append(kin)
            h = h + o
            buf = state_conv[l] if from_cache else jnp.zeros((x.shape[0], CONV_W - 1, D_FF), x.dtype)
            f, nbuf = conv_ffn(rmsnorm(h, g_ffn[l]), buf, w_gate[l], w_up[l], conv_w[l], conv_b[l], w_down[l])
            cs.append(nbuf)
            h = h + f
        return (rmsnorm(h, g_final), jnp.stack(ke, 1), jnp.stack(ve, 1), jnp.stack(ko, 1),
                jnp.stack(vo, 1), jnp.stack(kio, 1), jnp.stack(cs, 0))

    def to_pages(r):
        B, NL, S, W = r.shape
        return r.reshape(B, NL, S // PAGE_SIZE, PAGE_SIZE, W).transpose(0, 2, 1, 3, 4)

    pos_p = jnp.arange(x_prompt.shape[1], dtype=jnp.int32)
    pos_s = past_len + jnp.arange(x_sample.shape[1], dtype=jnp.int32)
    y_prompt, kep, vep, kop, vop, kiop, conv_p = trunk(x_prompt, pos_p, False)
    y_sample, k_even_s, v_even_s, k_odd_s, v_odd_s, kidx_odd_s, conv_s = trunk(x_sample, pos_s, True)
    k_even_p = to_pages(kep)
    v_even_p = to_pages(vep)
    k_odd_p = to_pages(kop)
    v_odd_p = to_pages(vop)
    kidx_odd_p = to_pages(kiop)
    return (y_prompt, y_sample, k_even_p, v_even_p, k_odd_p, v_odd_p, kidx_odd_p, conv_p,
            k_even_s, v_even_s, k_odd_s, v_odd_s, kidx_odd_s, conv_s)
```

```python
import functools
import math

import jax
import jax.numpy as jnp
from jax import lax
from jax.experimental import pallas as pl
from jax.experimental.pallas import tpu as pltpu

F32 = jnp.float32
BF16 = jnp.bfloat16

HEAD_DIM = 64
A_HEADS = 4
A_VDIM = 2 * HEAD_DIM
B_HEADS = 8
MOBA_BLOCK = 256
MOBA_TOPK = 3
C_HEADS = 16
C_KV_HEADS = 4
IDX_HEADS = 8
IDX_DIM = 64
DSA_TOPK = 256
CONV_W = 3
ROPE_THETA = 10000.0
EPS = 1e-6
PAGE_SIZE = 128
A_QK = A_HEADS * 2 * HEAD_DIM
A_V = A_HEADS * A_VDIM
B_W = B_HEADS * HEAD_DIM
C_Q = C_HEADS * HEAD_DIM
C_KV = C_KV_HEADS * HEAD_DIM
IDX_Q = IDX_HEADS * IDX_DIM
IDX_W_SCALE = IDX_Q ** -0.5
ATTN_SCALE = HEAD_DIM ** -0.5

LANES = 128
NEG = -1e30
VMEM_LIMIT = 56 * 1024 * 1024

EVEN_N = 2 * A_QK + A_V + 3 * B_W
ODD_N = C_Q + 2 * C_KV + IDX_Q + IDX_DIM + IDX_HEADS
ODD_NPAD = 2304
ODD_KI_COL = C_Q + 2 * C_KV + IDX_Q
PROJ_TN = 768


def _params(sem, vmem=VMEM_LIMIT):
    return pltpu.CompilerParams(dimension_semantics=sem, vmem_limit_bytes=vmem)


def _dot_nt(a, b):
    return lax.dot_general(a, b, (((1,), (1,)), ((), ())), preferred_element_type=F32)


def _dot(a, b):
    return jnp.dot(a, b, preferred_element_type=F32)


def _rms(x, g):
    return x * lax.rsqrt(jnp.mean(x * x, axis=-1, keepdims=True) + EPS) * g


def _inproj_kernel(x_ref, g_ref, w_ref, cos_ref, sin_ref, rmask_ref, s32_ref, s16_ref,
                   o32_ref, o16_ref, xn_ref):
    @pl.when(pl.program_id(1) == 0)
    def _():
        xn_ref[...] = _rms(x_ref[...], g_ref[...]).astype(BF16)

    y = _dot(xn_ref[...], w_ref[...])
    tn = y.shape[1]
    reps = tn // LANES
    cos = jnp.tile(cos_ref[...], (1, reps))
    sin = jnp.tile(sin_ref[...], (1, reps))
    lane = lax.broadcasted_iota(jnp.int32, y.shape, 1)
    first_half = (lane % HEAD_DIM) < (HEAD_DIM // 2)
    partner = jnp.where(first_half, pltpu.roll(y, tn - HEAD_DIM // 2, 1), pltpu.roll(y, HEAD_DIM // 2, 1))
    roped = y * cos + partner * sin
    out = jnp.where(rmask_ref[...] > 0.5, roped, y) * s32_ref[...]
    o32_ref[...] = out
    o16_ref[...] = (out * s16_ref[...]).astype(BF16)


def _inproj(x, g, w16, cos, sin, rmask, s32, s16, tm):
    m, d = x.shape
    n = w16.shape[1]
    tn = PROJ_TN
    tblocks = cos.shape[0] // tm
    return pl.pallas_call(
        _inproj_kernel,
        grid=(m // tm, n // tn),
        in_specs=[
            pl.BlockSpec((tm, d), lambda i, j: (i, 0)),
            pl.BlockSpec((1, d), lambda i, j: (0, 0)),
            pl.BlockSpec((d, tn), lambda i, j: (0, j)),
            pl.BlockSpec((tm, LANES), lambda i, j: (i % tblocks, 0)),
            pl.BlockSpec((tm, LANES), lambda i, j: (i % tblocks, 0)),
            pl.BlockSpec((1, tn), lambda i, j: (0, j)),
            pl.BlockSpec((1, tn), lambda i, j: (0, j)),
            pl.BlockSpec((1, tn), lambda i, j: (0, j)),
        ],
        out_specs=[
            pl.BlockSpec((tm, tn), lambda i, j: (i, j)),
            pl.BlockSpec((tm, tn), lambda i, j: (i, j)),
        ],
        out_shape=[jax.ShapeDtypeStruct((m, n), F32), jax.ShapeDtypeStruct((m, n), BF16)],
        scratch_shapes=[pltpu.VMEM((tm, d), BF16)],
        compiler_params=_params(("parallel", "arbitrary")),
        name="inproj",
    )(x, g, w16, cos, sin, rmask, s32, s16)


def _rope_tables(pos):
    half = HEAD_DIM // 2
    inv = ROPE_THETA ** (-jnp.arange(half, dtype=F32) * 2.0 / HEAD_DIM)
    ang = pos.astype(F32)[:, None] * inv[None, :]
    cos = jnp.tile(jnp.cos(ang), (1, LANES // half))
    sin = jnp.tile(jnp.concatenate([-jnp.sin(ang), jnp.sin(ang)], axis=1), (1, LANES // HEAD_DIM))
    return cos, sin


def _col_rows(n, rope_ranges, scale32, scale16):
    cols = jnp.arange(n)
    rmask = jnp.zeros((n,), F32)
    for lo, hi in rope_ranges:
        rmask = jnp.where((cols >= lo) & (cols < hi), 1.0, rmask)
    s32 = jnp.ones((n,), F32)
    for lo, hi, v in scale32:
        s32 = jnp.where((cols >= lo) & (cols < hi), v, s32)
    s16 = jnp.ones((n,), F32)
    for lo, hi, v in scale16:
        s16 = jnp.where((cols >= lo) & (cols < hi), v, s16)
    return rmask[None], s32[None], s16[None]


def _even_cols():
    o = A_QK + A_QK + A_V
    return _col_rows(EVEN_N, [(0, 2 * A_QK), (o, o + 2 * B_W)], [],
                     [(0, A_QK, ATTN_SCALE), (o, o + B_W, ATTN_SCALE)])


def _odd_cols():
    return _col_rows(ODD_NPAD, [(0, C_Q + C_KV), (C_Q + 2 * C_KV, ODD_KI_COL + IDX_DIM)],
                     [(ODD_KI_COL + IDX_DIM, ODD_N, IDX_W_SCALE)], [(0, C_Q, ATTN_SCALE)])


def _outproj_kernel(*refs, n_in):
    xs, ws, h_ref, o_ref = refs[:n_in], refs[n_in:2 * n_in], refs[2 * n_in], refs[2 * n_in + 1]
    acc = h_ref[...]
    for x_ref, w_ref in zip(xs, ws):
        acc = acc + _dot(x_ref[...], w_ref[...])
    o_ref[...] = acc


def _outproj(xs, ws, h, tm):
    m, d = h.shape
    n_in = len(xs)
    in_specs = [pl.BlockSpec((tm, x.shape[1]), lambda i: (i, 0)) for x in xs]
    in_specs += [pl.BlockSpec(w.shape, lambda i: (0, 0)) for w in ws]
    in_specs += [pl.BlockSpec((tm, d), lambda i: (i, 0))]
    return pl.pallas_call(
        functools.partial(_outproj_kernel, n_in=n_in),
        grid=(m // tm,),
        in_specs=in_specs,
        out_specs=pl.BlockSpec((tm, d), lambda i: (i, 0)),
        out_shape=jax.ShapeDtypeStruct((m, d), F32),
        compiler_params=_params(("parallel",)),
        name="outproj",
    )(*xs, *ws, h)


FFN_HALO = 16


def _ffn_kernel(h_ref, halo_ref, g_ref, wg_ref, wu_ref, cw_ref, cb_ref, wd_ref, gf_ref,
                o_ref, tail_ref, xn_ref, gext_ref, acc_ref, *, tiles_per_seq, final_norm):
    i, f = pl.program_id(0), pl.program_id(1)
    tm = h_ref.shape[0]

    @pl.when(f == 0)
    def _():
        xn_ref[0:FFN_HALO, :] = _rms(halo_ref[...], g_ref[...]).astype(BF16)
        xn_ref[FFN_HALO:, :] = _rms(h_ref[...], g_ref[...]).astype(BF16)
        acc_ref[...] = jnp.zeros_like(acc_ref)

    xn = xn_ref[...]
    gext = _dot(xn, wg_ref[...])
    first = (i % tiles_per_seq) == 0
    row = lax.broadcasted_iota(jnp.int32, gext.shape, 0)
    gext_ref[...] = jnp.where(jnp.logical_and(first, row < FFN_HALO), 0.0, gext)
    g = gext_ref[FFN_HALO:, :]
    p1 = gext_ref[pl.ds(FFN_HALO - 1, tm), :]
    p2 = gext_ref[pl.ds(FFN_HALO - 2, tm), :]
    cw = cw_ref[...]
    c = cb_ref[...] + p2 * cw[0:1] + p1 * cw[1:2] + g * cw[2:3]
    u = _dot(xn[FFN_HALO:], wu_ref[...])
    a = (c * jax.nn.sigmoid(c) * u).astype(BF16)
    acc_ref[...] += _dot(a, wd_ref[...])
    tail_ref[0] = g[tm - 8:, :]

    @pl.when(f == pl.num_programs(1) - 1)
    def _():
        out = h_ref[...] + acc_ref[...]
        if final_norm:
            out = _rms(out, gf_ref[...])
        o_ref[...] = out


def _ffn(h, g, wg, wu, cw, cb, wd, gf, seq_len, tm, tf, final_norm):
    m, d = h.shape
    ff = wg.shape[1]
    nt = m // tm
    hb = tm // FFN_HALO
    out, tail = pl.pallas_call(
        functools.partial(_ffn_kernel, tiles_per_seq=seq_len // tm, final_norm=final_norm),
        grid=(nt, ff // tf),
        in_specs=[
            pl.BlockSpec((tm, d), lambda i, f: (i, 0)),
            pl.BlockSpec((FFN_HALO, d), lambda i, f: (jnp.maximum(i * hb - 1, 0), 0)),
            pl.BlockSpec((1, d), lambda i, f: (0, 0)),
            pl.BlockSpec((d, tf), lambda i, f: (0, f)),
            pl.BlockSpec((d, tf), lambda i, f: (0, f)),
            pl.BlockSpec((CONV_W, tf), lambda i, f: (0, f)),
            pl.BlockSpec((1, tf), lambda i, f: (0, f)),
            pl.BlockSpec((tf, d), lambda i, f: (f, 0)),
            pl.BlockSpec((1, d), lambda i, f: (0, 0)),
        ],
        out_specs=[
            pl.BlockSpec((tm, d), lambda i, f: (i, 0)),
            pl.BlockSpec((1, 8, tf), lambda i, f: (i, 0, f)),
        ],
        out_shape=[jax.ShapeDtypeStruct((m, d), F32), jax.ShapeDtypeStruct((nt, 8, ff), F32)],
        scratch_shapes=[pltpu.VMEM((FFN_HALO + tm, d), BF16), pltpu.VMEM((FFN_HALO + tm, tf), F32),
                        pltpu.VMEM((tm, d), F32)],
        compiler_params=_params(("parallel", "arbitrary")),
        name="ffn",
    )(h, h, g, wg, wu, cw, cb, wd, gf)
    return out, tail


def _ffn_s_kernel(h_ref, b1_ref, b2_ref, g_ref, wg_ref, wu_ref, cw_ref, cb_ref, wd_ref, gf_ref,
                  o_ref, gout_ref, xn_ref, acc_ref, *, seq_len, final_norm):
    f = pl.program_id(0)

    @pl.when(f == 0)
    def _():
        xn_ref[...] = _rms(h_ref[...], g_ref[...]).astype(BF16)
        acc_ref[...] = jnp.zeros_like(acc_ref)

    xn = xn_ref[...]
    g = _dot(xn, wg_ref[...])
    t = lax.broadcasted_iota(jnp.int32, g.shape, 0) % seq_len
    p1 = jnp.where(t < 1, b1_ref[...], pltpu.roll(g, 1, 0))
    p2 = jnp.where(t < 2, b2_ref[...], pltpu.roll(g, 2, 0))
    cw = cw_ref[...]
    c = cb_ref[...] + p2 * cw[0:1] + p1 * cw[1:2] + g * cw[2:3]
    u = _dot(xn, wu_ref[...])
    a = (c * jax.nn.sigmoid(c) * u).astype(BF16)
    acc_ref[...] += _dot(a, wd_ref[...])
    gout_ref[...] = g

    @pl.when(f == pl.num_programs(0) - 1)
    def _():
        out = h_ref[...] + acc_ref[...]
        if final_norm:
            out = _rms(out, gf_ref[...])
        o_ref[...] = out


def _ffn_s(h, buf, g, wg, wu, cw, cb, wd, gf, seq_len, tf, final_norm):
    m, d = h.shape
    ff = wg.shape[1]
    nseq = m // seq_len
    zeros = jnp.zeros((nseq, seq_len - 2, ff), F32)
    b1 = jnp.concatenate([buf[:, 1:2], jnp.zeros((nseq, 1, ff), F32), zeros], axis=1).reshape(m, ff)
    b2 = jnp.concatenate([buf[:, 0:1], buf[:, 1:2], zeros], axis=1).reshape(m, ff)
    out, gout = pl.pallas_call(
        functools.partial(_ffn_s_kernel, seq_len=seq_len, final_norm=final_norm),
        grid=(ff // tf,),
        in_specs=[
            pl.BlockSpec((m, d), lambda f: (0, 0)),
            pl.BlockSpec((m, tf), lambda f: (0, f)),
            pl.BlockSpec((m, tf), lambda f: (0, f)),
            pl.BlockSpec((1, d), lambda f: (0, 0)),
            pl.BlockSpec((d, tf), lambda f: (0, f)),
            pl.BlockSpec((d, tf), lambda f: (0, f)),
            pl.BlockSpec((CONV_W, tf), lambda f: (0, f)),
            pl.BlockSpec((1, tf), lambda f: (0, f)),
            pl.BlockSpec((tf, d), lambda f: (f, 0)),
            pl.BlockSpec((1, d), lambda f: (0, 0)),
        ],
        out_specs=[
            pl.BlockSpec((m, d), lambda f: (0, 0)),
            pl.BlockSpec((m, tf), lambda f: (0, f)),
        ],
        out_shape=[jax.ShapeDtypeStruct((m, d), F32), jax.ShapeDtypeStruct((m, ff), F32)],
        scratch_shapes=[pltpu.VMEM((m, d), BF16), pltpu.VMEM((m, d), F32)],
        compiler_params=_params(("arbitrary",)),
        name="ffn_sample",
    )(h, b1, b2, g, wg, wu, cw, cb, wd, gf)
    return out, gout


def _flash_update(s, v, m_ref, l_ref, acc_ref, idx):
    m_prev = m_ref[idx]
    m_new = jnp.maximum(m_prev, jnp.max(s, axis=1, keepdims=True))
    alpha = jnp.exp(m_prev - m_new)
    p = jnp.exp(s - m_new)
    l_ref[idx] = alpha * l_ref[idx] + jnp.sum(p, axis=1, keepdims=True)
    acc_ref[idx] = alpha * acc_ref[idx] + _dot(p.astype(BF16), v)
    m_ref[idx] = m_new


def _diff_kernel(lam_ref, q_ref, k_ref, v_ref, g_ref, o_ref, m_ref, l_ref, acc_ref, *, tq, tk, out_scale):
    i, j = pl.program_id(1), pl.program_id(2)
    jlast = ((i + 1) * tq - 1) // tk

    @pl.when(j == 0)
    def _():
        m_ref[...] = jnp.full_like(m_ref, NEG)
        l_ref[...] = jnp.zeros_like(l_ref)
        acc_ref[...] = jnp.zeros_like(acc_ref)

    def step(masked):
        if masked:
            qpos = i * tq + lax.broadcasted_iota(jnp.int32, (tq, tk), 0)
            kpos = j * tk + lax.broadcasted_iota(jnp.int32, (tq, tk), 1)
            ok = kpos <= qpos
        for h in range(A_HEADS):
            v = v_ref[0, :, h * A_VDIM:(h + 1) * A_VDIM]
            for mp in range(2):
                c0 = h * A_VDIM + mp * HEAD_DIM
                s = _dot_nt(q_ref[0, :, c0:c0 + HEAD_DIM], k_ref[0, :, c0:c0 + HEAD_DIM])
                if masked:
                    s = jnp.where(ok, s, NEG)
                _flash_update(s, v, m_ref, l_ref, acc_ref, 2 * h + mp)

    crosses = (j + 1) * tk - 1 > i * tq

    @pl.when(jnp.logical_and(j <= jlast, crosses))
    def _():
        step(True)

    @pl.when(jnp.logical_and(j <= jlast, jnp.logical_not(crosses)))
    def _():
        step(False)

    @pl.when(j == jlast)
    def _():
        lam = lam_ref[0]
        for h in range(A_HEADS):
            o = acc_ref[2 * h] / l_ref[2 * h] - lam * (acc_ref[2 * h + 1] / l_ref[2 * h + 1])
            o = _rms(o, g_ref[...]) * out_scale
            o_ref[0, :, h * A_VDIM:(h + 1) * A_VDIM] = o.astype(BF16)


def _diff_attn(lam, qkv16, subln_g, out_scale, tq, tk):
    b, l, _ = qkv16.shape
    nq, nk = l // tq, l // tk

    def kv_map(col):
        return lambda bi, i, j, lam_ref: (bi, jnp.minimum(j, ((i + 1) * tq - 1) // tk), col)

    return pl.pallas_call(
        functools.partial(_diff_kernel, tq=tq, tk=tk, out_scale=out_scale),
        grid_spec=pltpu.PrefetchScalarGridSpec(
            num_scalar_prefetch=1,
            grid=(b, nq, nk),
            in_specs=[
                pl.BlockSpec((1, tq, A_QK), lambda bi, i, j, lam_ref: (bi, i, 0)),
                pl.BlockSpec((1, tk, A_QK), kv_map(1)),
                pl.BlockSpec((1, tk, A_V), kv_map(2)),
                pl.BlockSpec((1, A_VDIM), lambda bi, i, j, lam_ref: (0, 0)),
            ],
            out_specs=pl.BlockSpec((1, tq, A_V), lambda bi, i, j, lam_ref: (bi, i, 0)),
            scratch_shapes=[pltpu.VMEM((2 * A_HEADS, tq, 1), F32), pltpu.VMEM((2 * A_HEADS, tq, 1), F32),
                            pltpu.VMEM((2 * A_HEADS, tq, A_VDIM), F32)],
        ),
        out_shape=jax.ShapeDtypeStruct((b, l, A_V), BF16),
        compiler_params=_params(("parallel", "parallel", "arbitrary")),
        name="diff_attn",
    )(lam, qkv16, qkv16, qkv16, subln_g)


def _blockmean_kernel(k_ref, o_ref):
    nb = o_ref.shape[1]
    for n in range(nb):
        o_ref[0, n:n + 1, :] = jnp.mean(k_ref[0, n * MOBA_BLOCK:(n + 1) * MOBA_BLOCK, :], axis=0, keepdims=True)


def _blockmean(x32, col_block, width, nb_step):
    b, l, _ = x32.shape
    nb = l // MOBA_BLOCK
    return pl.pallas_call(
        _blockmean_kernel,
        grid=(b, nb // nb_step),
        in_specs=[pl.BlockSpec((1, nb_step * MOBA_BLOCK, width), lambda bi, i: (bi, i, col_block))],
        out_specs=pl.BlockSpec((1, nb_step, width), lambda bi, i: (bi, i, 0)),
        out_shape=jax.ShapeDtypeStruct((b, nb, width), F32),
        compiler_params=_params(("parallel", "parallel")),
        name="blockmean",
    )(x32)


def _top3_select(gate, n_valid):
    nb = gate.shape[1]
    idx = lax.broadcasted_iota(jnp.int32, gate.shape, 1)
    sel = jnp.zeros(gate.shape, F32)
    for r in range(MOBA_TOPK):
        mx = jnp.max(gate, axis=1, keepdims=True)
        first = jnp.min(jnp.where(gate == mx, idx, nb), axis=1, keepdims=True)
        pick = idx == first
        sel = jnp.where(jnp.logical_and(pick, r < n_valid), 1.0, sel)
        gate = jnp.where(pick, -jnp.inf, gate)
    return sel


def _dot_nt_f32(a, b):
    return lax.dot_general(a, b, (((1,), (1,)), ((), ())), preferred_element_type=F32,
                           precision=lax.Precision.HIGHEST)


def _moba_kernel(q_ref, q32_ref, k_ref, v_ref, km_ref, o_ref, sel_ref, m_ref, l_ref, acc_ref, *, sub):
    i, j = pl.program_id(1), pl.program_id(2)
    tq = MOBA_BLOCK
    nb = km_ref.shape[1]
    jlast = i // sub

    @pl.when(j == 0)
    def _():
        m_ref[...] = jnp.full_like(m_ref, NEG)
        l_ref[...] = jnp.zeros_like(l_ref)
        acc_ref[...] = jnp.zeros_like(acc_ref)
        blk = lax.broadcasted_iota(jnp.int32, (tq, nb), 1)
        for h in range(B_HEADS):
            c0 = h * HEAD_DIM
            gate = _dot_nt_f32(q32_ref[0, :, c0:c0 + HEAD_DIM], km_ref[0, :, c0:c0 + HEAD_DIM])
            gate = jnp.where(blk < i, gate, -jnp.inf)
            sel_ref[h] = _top3_select(gate, i)

    def sub_block(r, own):
        n = j * sub + r
        rows = slice(r * MOBA_BLOCK, (r + 1) * MOBA_BLOCK)
        if own:
            ok = (lax.broadcasted_iota(jnp.int32, (tq, MOBA_BLOCK), 1)
                  <= lax.broadcasted_iota(jnp.int32, (tq, MOBA_BLOCK), 0))
        else:
            blk = lax.broadcasted_iota(jnp.int32, (tq, nb), 1)
        for h in range(B_HEADS):
            c0 = h * HEAD_DIM
            s = _dot_nt(q_ref[0, :, c0:c0 + HEAD_DIM], k_ref[0, rows, c0:c0 + HEAD_DIM])
            if not own:
                ok = jnp.max(jnp.where(blk == n, sel_ref[h], 0.0), axis=1, keepdims=True) > 0.5
            s = jnp.where(ok, s, NEG)
            _flash_update(s, v_ref[0, rows, c0:c0 + HEAD_DIM], m_ref, l_ref, acc_ref, h)

    for r in range(sub):
        n = j * sub + r

        @pl.when(jnp.logical_and(j <= jlast, n < i))
        def _():
            sub_block(r, False)

        @pl.when(n == i)
        def _():
            sub_block(r, True)

    @pl.when(j == jlast)
    def _():
        for h in range(B_HEADS):
            o_ref[0, :, h * HEAD_DIM:(h + 1) * HEAD_DIM] = (acc_ref[h] / l_ref[h]).astype(BF16)


def _moba_attn(qkv16, qkv32, kmean, sub):
    b, l, _ = qkv16.shape
    nb = l // MOBA_BLOCK
    tk = sub * MOBA_BLOCK
    qcol = (2 * A_QK + A_V) // B_W

    def kv_map(col):
        return lambda bi, i, j: (bi, jnp.minimum(j, i // sub), col)

    return pl.pallas_call(
        functools.partial(_moba_kernel, sub=sub),
        grid=(b, nb, l // tk),
        in_specs=[
            pl.BlockSpec((1, MOBA_BLOCK, B_W), lambda bi, i, j: (bi, i, qcol)),
            pl.BlockSpec((1, MOBA_BLOCK, B_W), lambda bi, i, j: (bi, i, qcol)),
            pl.BlockSpec((1, tk, B_W), kv_map(qcol + 1)),
            pl.BlockSpec((1, tk, B_W), kv_map(qcol + 2)),
            pl.BlockSpec((1, nb, B_W), lambda bi, i, j: (bi, 0, 0)),
        ],
        out_specs=pl.BlockSpec((1, MOBA_BLOCK, B_W), lambda bi, i, j: (bi, i, 0)),
        out_shape=jax.ShapeDtypeStruct((b, l, B_W), BF16),
        scratch_shapes=[pltpu.VMEM((B_HEADS, MOBA_BLOCK, nb), F32), pltpu.VMEM((B_HEADS, MOBA_BLOCK, 1), F32),
                        pltpu.VMEM((B_HEADS, MOBA_BLOCK, 1), F32), pltpu.VMEM((B_HEADS, MOBA_BLOCK, HEAD_DIM), F32)],
        compiler_params=_params(("parallel", "parallel", "arbitrary")),
        name="moba_attn",
    )(qkv16, qkv32, qkv16, qkv16, kmean)


INT_MIN = -2 ** 31


def _order_key(score):
    bits = pltpu.bitcast(jnp.where(score == 0.0, 0.0, score), jnp.int32)
    return bits ^ ((bits >> 31) & 0x7FFFFFFF)


def _lane_fold(x):
    part = x[:, 0:LANES]
    for s in range(1, x.shape[1] // LANES):
        part = part + x[:, s * LANES:(s + 1) * LANES]
    return part


def _kth_largest_key(key_ref, nch, kk, rows):
    def count_ge(cand):
        def body(c, acc):
            return acc + _lane_fold(jnp.where(key_ref[c] >= cand, 1.0, 0.0))
        acc = lax.fori_loop(0, nch, body, jnp.zeros((rows, LANES), F32))
        return jnp.sum(acc, axis=1, keepdims=True)

    def search(_, carry):
        t_u, bit = carry
        cand_u = t_u | bit
        cnt = count_ge(cand_u ^ INT_MIN)
        return jnp.where(cnt >= kk, cand_u, t_u), lax.shift_right_logical(bit, 1)

    t_u, _ = lax.fori_loop(0, 32, search, (jnp.zeros((rows, 1), jnp.int32), jnp.full((rows, 1), INT_MIN, jnp.int32)))
    return t_u ^ INT_MIN


def _topk_bias(key_ref, bias_ref, tri_ref, nch, kk, rows):
    tc = key_ref.shape[2]
    t = _kth_largest_key(key_ref, nch, kk, rows)

    def count2(c, carry):
        gt, eq = carry
        blk = key_ref[c]
        return (gt + _lane_fold(jnp.where(blk > t, 1.0, 0.0)), eq + _lane_fold(jnp.where(blk == t, 1.0, 0.0)))

    zero = jnp.zeros((rows, LANES), F32)
    gt, eq = lax.fori_loop(0, nch, count2, (zero, zero))
    need = kk - jnp.sum(gt, axis=1, keepdims=True)
    n_eq = jnp.sum(eq, axis=1, keepdims=True)
    ties = jnp.max(n_eq - need) > 0.5

    @pl.when(jnp.logical_not(ties))
    def _():
        def body(c, _):
            bias_ref[c] = jnp.where(key_ref[c] >= t, 0.0, NEG)
            return 0
        lax.fori_loop(0, nch, body, 0)

    @pl.when(ties)
    def _():
        r = lax.broadcasted_iota(jnp.int32, (tc, tc), 0)
        cidx = lax.broadcasted_iota(jnp.int32, (tc, tc), 1)
        tri_ref[...] = jnp.where(r <= cidx, 1.0, 0.0).astype(BF16)

        def body(c, seen):
            blk = key_ref[c]
            eqm = blk == t
            rank = seen + _dot(jnp.where(eqm, 1.0, 0.0).astype(BF16), tri_ref[...])
            take = jnp.logical_or(blk > t, jnp.logical_and(eqm, rank <= need))
            bias_ref[c] = jnp.where(take, 0.0, NEG)
            return seen + jnp.sum(jnp.where(eqm, 1.0, 0.0), axis=1, keepdims=True)
        lax.fori_loop(0, nch, body, jnp.zeros((rows, 1), F32))


def _dsa_kernel(q_ref, qi_ref, wi_ref, k_ref, v_ref, ki_ref, o_ref,
                key_ref, bias_ref, tri_ref, m_ref, l_ref, acc_ref, *, tq, tc, topk):
    i = pl.program_id(1)
    nch = ((i + 1) * tq - 1) // tc + 1
    qpos = i * tq + lax.broadcasted_iota(jnp.int32, (tq, 1), 0)
    kk = jnp.minimum(topk, qpos + 1).astype(F32)

    def score_chunk(c, _):
        r0 = pl.multiple_of(c * tc, tc)
        kic = ki_ref[0, pl.ds(r0, tc), 0:IDX_DIM]
        score = jnp.zeros((tq, tc), F32)
        for h in range(IDX_HEADS):
            lg = _dot_nt(qi_ref[0, :, h * IDX_DIM:(h + 1) * IDX_DIM], kic)
            score = score + wi_ref[0, :, IDX_DIM + h:IDX_DIM + h + 1] * jnp.maximum(lg, 0.0)
        kpos = r0 + lax.broadcasted_iota(jnp.int32, (tq, tc), 1)
        key_ref[c] = jnp.where(kpos <= qpos, _order_key(score), INT_MIN)
        return 0

    lax.fori_loop(0, nch, score_chunk, 0)
    _topk_bias(key_ref, bias_ref, tri_ref, nch, kk, tq)

    m_ref[...] = jnp.full_like(m_ref, NEG)
    l_ref[...] = jnp.zeros_like(l_ref)
    acc_ref[...] = jnp.zeros_like(acc_ref)
    group = C_HEADS // C_KV_HEADS

    def attend(c, _):
        r0 = pl.multiple_of(c * tc, tc)
        bias = bias_ref[c]
        for g in range(C_KV_HEADS):
            kc = k_ref[0, pl.ds(r0, tc), g * HEAD_DIM:(g + 1) * HEAD_DIM]
            vc = v_ref[0, pl.ds(r0, tc), g * HEAD_DIM:(g + 1) * HEAD_DIM]
            for hh in range(group):
                h = g * group + hh
                s = _dot_nt(q_ref[0, :, h * HEAD_DIM:(h + 1) * HEAD_DIM], kc) + bias
                _flash_update(s, vc, m_ref, l_ref, acc_ref, h)
        return 0

    lax.fori_loop(0, nch, attend, 0)
    for h in range(C_HEADS):
        o_ref[0, :, h * HEAD_DIM:(h + 1) * HEAD_DIM] = (acc_ref[h] / l_ref[h]).astype(BF16)


def _dsa_attn(qkv16, qkv32, tq, tc):
    b, l, _ = qkv16.shape
    topk = min(DSA_TOPK, l // 4)
    nc = l // tc
    ki_blk = ODD_KI_COL // LANES
    return pl.pallas_call(
        functools.partial(_dsa_kernel, tq=tq, tc=tc, topk=topk),
        grid=(b, l // tq),
        in_specs=[
            pl.BlockSpec((1, tq, C_Q), lambda bi, i: (bi, i, 0)),
            pl.BlockSpec((1, tq, IDX_Q), lambda bi, i: (bi, i, (C_Q + 2 * C_KV) // IDX_Q)),
            pl.BlockSpec((1, tq, LANES), lambda bi, i: (bi, i, ki_blk)),
            pl.BlockSpec((1, l, C_KV), lambda bi, i: (bi, 0, C_Q // C_KV)),
            pl.BlockSpec((1, l, C_KV), lambda bi, i: (bi, 0, C_Q // C_KV + 1)),
            pl.BlockSpec((1, l, LANES), lambda bi, i: (bi, 0, ki_blk)),
        ],
        out_specs=pl.BlockSpec((1, tq, C_Q), lambda bi, i: (bi, i, 0)),
        out_shape=jax.ShapeDtypeStruct((b, l, C_Q), BF16),
        scratch_shapes=[pltpu.VMEM((nc, tq, tc), jnp.int32), pltpu.VMEM((nc, tq, tc), F32),
                        pltpu.VMEM((tc, tc), BF16),
                        pltpu.VMEM((C_HEADS, tq, 1), F32), pltpu.VMEM((C_HEADS, tq, 1), F32),
                        pltpu.VMEM((C_HEADS, tq, HEAD_DIM), F32)],
        compiler_params=_params(("parallel", "arbitrary")),
        name="dsa_attn",
    )(qkv16, qkv16, qkv32, qkv16, qkv16, qkv16)


NEW_PAD = 16
EVEN_PAGES = MOBA_BLOCK // PAGE_SIZE


def _fold_heads(x, n_groups, n_new):
    return jnp.sum(x.reshape(n_groups, n_new, x.shape[1]), axis=0)


def _dec_even_kernel(pt_ref, lam_ref, qbd_ref, *refs, n_new, n_blocks, out_scale):
    kpages = refs[:EVEN_PAGES]
    vpages = refs[EVEN_PAGES:2 * EVEN_PAGES]
    knew_ref, vnew_ref, g_ref, o_ref, ma_ref, la_ref, acca_ref, mb_ref, lb_ref, accb_ref, km_ref = refs[2 * EVEN_PAGES:]
    p = pl.program_id(1)
    ra = 8 * n_new
    qbd = qbd_ref[0]

    @pl.when(p == 0)
    def _():
        ma_ref[...] = jnp.full_like(ma_ref, NEG)
        la_ref[...] = jnp.zeros_like(la_ref)
        acca_ref[...] = jnp.zeros_like(acca_ref)
        km_ref[...] = jnp.zeros_like(km_ref)

    def attend(k16, v16, mask, blk):
        s = _dot_nt(qbd, k16)
        if mask is not None:
            s = jnp.where(mask, s, NEG)
        sa, sb = s[:ra], s[ra:]
        m_prev = ma_ref[...]
        m_new = jnp.maximum(m_prev, jnp.max(sa, axis=1, keepdims=True))
        alpha = jnp.exp(m_prev - m_new)
        pa = jnp.exp(sa - m_new)
        la_ref[...] = alpha * la_ref[...] + jnp.sum(pa, axis=1, keepdims=True)
        acca_ref[...] = alpha * acca_ref[...] + _dot(pa.astype(BF16), v16[:, :A_V])
        ma_ref[...] = m_new
        mb = jnp.max(sb, axis=1, keepdims=True)
        pb = jnp.exp(sb - mb)
        mb_ref[blk] = mb
        lb_ref[blk] = jnp.sum(pb, axis=1, keepdims=True)
        accb_ref[blk] = _dot(pb.astype(BF16), v16[:, A_V:])

    @pl.when(p < n_blocks)
    def _():
        k32 = jnp.concatenate([r[...] for r in kpages], axis=0)
        v32 = jnp.concatenate([r[...] for r in vpages], axis=0)
        km_ref[pl.ds(p, 1), :] = jnp.mean(k32[:, A_QK:], axis=0, keepdims=True)
        attend(k32.astype(BF16), v32.astype(BF16), None, p)

    @pl.when(p == n_blocks)
    def _():
        rows = 16 * n_new
        t = lax.broadcasted_iota(jnp.int32, (rows, NEW_PAD), 0) % n_new
        j = lax.broadcasted_iota(jnp.int32, (rows, NEW_PAD), 1)
        attend(knew_ref[0], vnew_ref[0], j <= t, n_blocks)

        lam = lam_ref[0]
        r = acca_ref[...] / la_ref[...]
        m_idx = lax.broadcasted_iota(jnp.int32, r.shape, 0) // n_new
        col = lax.broadcasted_iota(jnp.int32, r.shape, 1)
        coef = jnp.where(m_idx % 2 == 0, 1.0, -lam)
        oa = _fold_heads(jnp.where(col // A_VDIM == m_idx // 2, r * coef, 0.0), 8, n_new)
        for h in range(A_HEADS):
            cs = slice(h * A_VDIM, (h + 1) * A_VDIM)
            o_ref[0, :, cs] = (_rms(oa[:, cs], g_ref[...]) * out_scale).astype(BF16)

        nbp = km_ref.shape[0]
        gate = _dot_nt_f32(qbd[ra:, A_QK:].astype(F32), km_ref[...])
        blk = lax.broadcasted_iota(jnp.int32, gate.shape, 1)
        sel = _top3_select(jnp.where(blk < n_blocks, gate, -jnp.inf), n_blocks)
        m_run, l_run, acc = mb_ref[n_blocks], lb_ref[n_blocks], accb_ref[n_blocks]
        for n in range(n_blocks):
            chosen = sel[:, n:n + 1] > 0.5
            m_n = mb_ref[n]
            m_new = jnp.where(chosen, jnp.maximum(m_run, m_n), m_run)
            a_old = jnp.exp(m_run - m_new)
            a_n = jnp.where(chosen, jnp.exp(m_n - m_new), 0.0)
            l_run = a_old * l_run + a_n * lb_ref[n]
            acc = a_old * acc + a_n * accb_ref[n]
            m_run = m_new
        r = acc / l_run
        h_idx = lax.broadcasted_iota(jnp.int32, r.shape, 0) // n_new
        col = lax.broadcasted_iota(jnp.int32, r.shape, 1)
        ob = _fold_heads(jnp.where(col // HEAD_DIM == h_idx, r, 0.0), 8, n_new)
        o_ref[0, :, A_V:] = ob.astype(BF16)


def _block_diag_rows(q, n_maps, width):
    s, t, c = q.shape
    keep = (jnp.arange(c)[None, :] // width) == jnp.arange(n_maps)[:, None]
    return jnp.where(keep[None, :, None, :], q[:, None, :, :], 0).reshape(s, n_maps * t, c).astype(q.dtype)


def _dec_even(page_table, lam, qbd, cache_k, cache_v, layer, knew, vnew, subln_g, out_scale):
    nseq, n_pages = page_table.shape
    n_new = qbd.shape[1] // 16
    n_blocks = n_pages // EVEN_PAGES
    width = cache_k.shape[-1]

    def page_spec(r):
        return pl.BlockSpec((None, None, PAGE_SIZE, width),
                            lambda s, p, pt, lm: (pt[s, jnp.minimum(p, n_blocks - 1) * EVEN_PAGES + r], layer, 0, 0))

    seq_spec = lambda shape: pl.BlockSpec((1,) + shape, lambda s, p, pt, lm: (s, 0, 0))
    ra = 8 * n_new
    nbp = -(-(n_blocks + 1) // 8) * 8
    return pl.pallas_call(
        functools.partial(_dec_even_kernel, n_new=n_new, n_blocks=n_blocks, out_scale=out_scale),
        grid_spec=pltpu.PrefetchScalarGridSpec(
            num_scalar_prefetch=2,
            grid=(nseq, n_blocks + 1),
            in_specs=[seq_spec((16 * n_new, width))]
            + [page_spec(r) for r in range(EVEN_PAGES)] + [page_spec(r) for r in range(EVEN_PAGES)]
            + [seq_spec((NEW_PAD, width)), seq_spec((NEW_PAD, width)),
               pl.BlockSpec((1, A_VDIM), lambda s, p, pt, lm: (0, 0))],
            out_specs=seq_spec((n_new, A_V + B_W)),
            scratch_shapes=[pltpu.VMEM((ra, 1), F32), pltpu.VMEM((ra, 1), F32), pltpu.VMEM((ra, A_V), F32),
                            pltpu.VMEM((n_blocks + 1, ra, 1), F32), pltpu.VMEM((n_blocks + 1, ra, 1), F32),
                            pltpu.VMEM((n_blocks + 1, ra, B_W), F32), pltpu.VMEM((nbp, B_W), F32)],
        ),
        out_shape=jax.ShapeDtypeStruct((nseq, n_new, A_V + B_W), BF16),
        compiler_params=_params(("parallel", "arbitrary")),
        name="dec_even",
    )(page_table, lam, qbd, *([cache_k] * EVEN_PAGES), *([cache_v] * EVEN_PAGES), knew, vnew, subln_g)


ODD_PAGES = 8
ODD_CHUNK = ODD_PAGES * PAGE_SIZE


def _dec_select_kernel(pt_ref, qi_ref, wi_ref, *refs, n_new, n_chunks, topk):
    pages = refs[:ODD_PAGES]
    kinew_ref, bias_ref, key_ref, tri_ref = refs[ODD_PAGES:]
    p = pl.program_id(1)
    qi = qi_ref[0]
    wi = wi_ref[0]

    def scores(ki16):
        lg = _dot_nt(qi, ki16)
        return _fold_heads(wi * jnp.maximum(lg, 0.0), IDX_HEADS, n_new)

    @pl.when(p < n_chunks)
    def _():
        for r in range(ODD_PAGES):
            key_ref[p, :, r * PAGE_SIZE:(r + 1) * PAGE_SIZE] = _order_key(scores(pages[r][...].astype(BF16)))

    @pl.when(p == n_chunks)
    def _():
        sc = scores(kinew_ref[0])
        t = lax.broadcasted_iota(jnp.int32, sc.shape, 0)
        j = lax.broadcasted_iota(jnp.int32, sc.shape, 1)
        key_ref[n_chunks] = jnp.full((n_new, ODD_CHUNK), INT_MIN, jnp.int32)
        key_ref[n_chunks, :, 0:PAGE_SIZE] = jnp.where(j <= t, _order_key(sc), INT_MIN)
        qpos = n_chunks * ODD_CHUNK + lax.broadcasted_iota(jnp.int32, (n_new, 1), 0)
        kk = jnp.minimum(topk, qpos + 1).astype(F32)
        _topk_bias(key_ref, bias_ref.at[0], tri_ref, n_chunks + 1, kk, n_new)


def _dec_select(page_table, qi, wi, cache_ki, layer, kinew):
    nseq, n_pages = page_table.shape
    n_new = qi.shape[1] // IDX_HEADS
    n_chunks = n_pages // ODD_PAGES
    topk = min(DSA_TOPK, (n_pages * PAGE_SIZE + n_new) // 4)

    def page_spec(r):
        return pl.BlockSpec((None, None, PAGE_SIZE, IDX_DIM),
                            lambda s, p, pt: (pt[s, jnp.minimum(p, n_chunks - 1) * ODD_PAGES + r], layer, 0, 0))

    return pl.pallas_call(
        functools.partial(_dec_select_kernel, n_new=n_new, n_chunks=n_chunks, topk=topk),
        grid_spec=pltpu.PrefetchScalarGridSpec(
            num_scalar_prefetch=1,
            grid=(nseq, n_chunks + 1),
            in_specs=[pl.BlockSpec((1, IDX_HEADS * n_new, IDX_DIM), lambda s, p, pt: (s, 0, 0)),
                      pl.BlockSpec((1, IDX_HEADS * n_new, 1), lambda s, p, pt: (s, 0, 0))]
            + [page_spec(r) for r in range(ODD_PAGES)]
            + [pl.BlockSpec((1, PAGE_SIZE, IDX_DIM), lambda s, p, pt: (s, 0, 0))],
            out_specs=pl.BlockSpec((1, n_chunks + 1, n_new, ODD_CHUNK), lambda s, p, pt: (s, 0, 0, 0)),
            scratch_shapes=[pltpu.VMEM((n_chunks + 1, n_new, ODD_CHUNK), jnp.int32),
                            pltpu.VMEM((ODD_CHUNK, ODD_CHUNK), BF16)],
        ),
        out_shape=jax.ShapeDtypeStruct((nseq, n_chunks + 1, n_new, ODD_CHUNK), F32),
        compiler_params=_params(("parallel", "arbitrary")),
        name="dec_select",
    )(page_table, qi, wi, *([cache_ki] * ODD_PAGES), kinew)


def _dec_odd_kernel(pt_ref, qbd_ref, bias_ref, *refs, n_new, n_chunks):
    kpages = refs[:ODD_PAGES]
    vpages = refs[ODD_PAGES:2 * ODD_PAGES]
    knew_ref, vnew_ref, o_ref, m_ref, l_ref, acc_ref = refs[2 * ODD_PAGES:]
    p = pl.program_id(1)
    qbd = qbd_ref[0]

    @pl.when(p == 0)
    def _():
        m_ref[...] = jnp.full_like(m_ref, NEG)
        l_ref[...] = jnp.zeros_like(l_ref)
        acc_ref[...] = jnp.zeros_like(acc_ref)

    def attend(k16, v16, bias):
        s = _dot_nt(qbd, k16) + jnp.tile(bias, (C_HEADS, 1))
        m_prev = m_ref[...]
        m_new = jnp.maximum(m_prev, jnp.max(s, axis=1, keepdims=True))
        alpha = jnp.exp(m_prev - m_new)
        pr = jnp.exp(s - m_new)
        l_ref[...] = alpha * l_ref[...] + jnp.sum(pr, axis=1, keepdims=True)
        acc_ref[...] = alpha * acc_ref[...] + _dot(pr.astype(BF16), v16)
        m_ref[...] = m_new

    @pl.when(p < n_chunks)
    def _():
        k16 = jnp.concatenate([r[...] for r in kpages], axis=0).astype(BF16)
        v16 = jnp.concatenate([r[...] for r in vpages], axis=0).astype(BF16)
        attend(k16, v16, bias_ref[0, 0])

    @pl.when(p == n_chunks)
    def _():
        attend(knew_ref[0], vnew_ref[0], bias_ref[0, 0, :, 0:NEW_PAD])
        r = acc_ref[...] / l_ref[...]
        group = C_HEADS // C_KV_HEADS
        for h in range(C_HEADS):
            g = h // group
            o_ref[0, :, h * HEAD_DIM:(h + 1) * HEAD_DIM] = (
                r[h * n_new:(h + 1) * n_new, g * HEAD_DIM:(g + 1) * HEAD_DIM].astype(BF16))


def _dec_odd(page_table, qbd, bias, cache_k, cache_v, layer, knew, vnew):
    nseq, n_pages = page_table.shape
    n_new = qbd.shape[1] // C_HEADS
    n_chunks = n_pages // ODD_PAGES

    def page_spec(r):
        return pl.BlockSpec((None, None, PAGE_SIZE, C_KV),
                            lambda s, p, pt: (pt[s, jnp.minimum(p, n_chunks - 1) * ODD_PAGES + r], layer, 0, 0))

    seq_spec = lambda shape: pl.BlockSpec((1,) + shape, lambda s, p, pt: (s, 0, 0))
    rows = C_HEADS * n_new
    return pl.pallas_call(
        functools.partial(_dec_odd_kernel, n_new=n_new, n_chunks=n_chunks),
        grid_spec=pltpu.PrefetchScalarGridSpec(
            num_scalar_prefetch=1,
            grid=(nseq, n_chunks + 1),
            in_specs=[seq_spec((rows, C_KV)),
                      pl.BlockSpec((1, 1, n_new, ODD_CHUNK), lambda s, p, pt: (s, p, 0, 0))]
            + [page_spec(r) for r in range(ODD_PAGES)] + [page_spec(r) for r in range(ODD_PAGES)]
            + [seq_spec((NEW_PAD, C_KV)), seq_spec((NEW_PAD, C_KV))],
            out_specs=seq_spec((n_new, C_Q)),
            scratch_shapes=[pltpu.VMEM((rows, 1), F32), pltpu.VMEM((rows, 1), F32), pltpu.VMEM((rows, C_KV), F32)],
        ),
        out_shape=jax.ShapeDtypeStruct((nseq, n_new, C_Q), BF16),
        compiler_params=_params(("parallel", "arbitrary")),
        name="dec_odd",
    )(page_table, qbd, bias, *([cache_k] * ODD_PAGES), *([cache_v] * ODD_PAGES), knew, vnew)


ROW_TILE = 1024
FFN_TF = 256


def _pad_rows(a, rows):
    return jnp.pad(a, ((0, 0), (0, rows - a.shape[1]), (0, 0)))


def _gqa_rows(q):
    s, t, nh, d = q.shape
    group = nh // C_KV_HEADS
    keep = (jnp.arange(nh)[:, None] // group) == jnp.arange(C_KV_HEADS)[None, :]
    qh = jnp.swapaxes(q, 1, 2)
    out = jnp.where(keep[None, :, None, :, None], qh[:, :, :, None, :], 0)
    return out.reshape(s, nh * t, C_KV_HEADS * d).astype(q.dtype)


def kernel(x_prompt, x_sample, cache_k_even, cache_v_even, cache_k_odd, cache_v_odd, cache_kidx_odd, state_conv,
           page_table, g_mix, g_ffn, g_final, w_in_even, w_out_even, lam_even, subln_even, w_in_odd, w_out_odd,
           w_gate, w_up, conv_w, conv_b, w_down):
    bsz, seq, d = x_prompt.shape
    nseq, n_new, _ = x_sample.shape
    depth = g_mix.shape[0]
    ff = w_gate.shape[-1]
    past_len = page_table.shape[1] * PAGE_SIZE
    bf = lambda a: a.astype(BF16)

    cos_p, sin_p = _rope_tables(jnp.arange(seq, dtype=jnp.int32))
    cos_s, sin_s = _rope_tables(jnp.tile(past_len + jnp.arange(n_new, dtype=jnp.int32), nseq))
    ecols, ocols = _even_cols(), _odd_cols()
    qb0 = 2 * A_QK + A_V
    tm = min(ROW_TILE, seq)
    ms = nseq * n_new

    hp = x_prompt.reshape(bsz * seq, d)
    hs = x_sample.reshape(ms, d)
    ke_p, ve_p, ko_p, vo_p, kio_p, cs_p = [], [], [], [], [], []
    ke_s, ve_s, ko_s, vo_s, kio_s, cs_s = [], [], [], [], [], []
    for layer in range(depth):
        e = layer // 2
        g = g_mix[layer][None]
        if layer % 2 == 0:
            w16, wo = bf(w_in_even[e]), bf(w_out_even[e])
            lam_init = 0.8 - 0.6 * math.exp(-0.3 * layer)
            lp = lam_even[e].astype(F32)
            lam = (jnp.exp(jnp.sum(lp[0] * lp[1])) - jnp.exp(jnp.sum(lp[2] * lp[3])) + lam_init).reshape(1)
            sg = subln_even[e][None]
            out_scale = 1.0 - lam_init

            p32, p16 = _inproj(hp, g, w16, cos_p, sin_p, *ecols, tm=tm)
            p32, p16 = p32.reshape(bsz, seq, -1), p16.reshape(bsz, seq, -1)
            oa = _diff_attn(lam, p16, sg, out_scale, tq=256, tk=512)
            kmean = _blockmean(p32, (qb0 + B_W) // B_W, B_W, nb_step=min(8, seq // MOBA_BLOCK))
            ob = _moba_attn(p16, p32, kmean, sub=2)
            hp = _outproj([oa.reshape(bsz * seq, -1), ob.reshape(bsz * seq, -1)], [wo[:A_V], wo[A_V:]], hp, tm=tm)
            ke_p.append(jnp.concatenate([p32[..., A_QK:2 * A_QK], p32[..., qb0 + B_W:qb0 + 2 * B_W]], -1))
            ve_p.append(jnp.concatenate([p32[..., 2 * A_QK:qb0], p32[..., qb0 + 2 * B_W:]], -1))

            s32, s16 = _inproj(hs, g, w16, cos_s, sin_s, *ecols, tm=ms)
            s32, s16 = s32.reshape(nseq, n_new, -1), s16.reshape(nseq, n_new, -1)
            qbd = _block_diag_rows(jnp.concatenate([s16[..., :A_QK], s16[..., qb0:qb0 + B_W]], -1), 16, HEAD_DIM)
            knew = jnp.concatenate([s16[..., A_QK:2 * A_QK], s16[..., qb0 + B_W:qb0 + 2 * B_W]], -1)
            vnew = jnp.concatenate([s16[..., 2 * A_QK:qb0], s16[..., qb0 + 2 * B_W:]], -1)
            o = _dec_even(page_table, lam, qbd, cache_k_even, cache_v_even, e,
                          _pad_rows(knew, NEW_PAD), _pad_rows(vnew, NEW_PAD), sg, out_scale)
            hs = _outproj([o.reshape(ms, -1)], [wo], hs, tm=ms)
            ke_s.append(jnp.concatenate([s32[..., A_QK:2 * A_QK], s32[..., qb0 + B_W:qb0 + 2 * B_W]], -1))
            ve_s.append(jnp.concatenate([s32[..., 2 * A_QK:qb0], s32[..., qb0 + 2 * B_W:]], -1))
        else:
            w16 = bf(jnp.pad(w_in_odd[e], ((0, 0), (0, ODD_NPAD - ODD_N))))
            wo = bf(w_out_odd[e])
            kc, vc, kic, wic = C_Q, C_Q + C_KV, ODD_KI_COL, ODD_KI_COL + IDX_DIM

            p32, p16 = _inproj(hp, g, w16, cos_p, sin_p, *ocols, tm=tm)
            p32, p16 = p32.reshape(bsz, seq, -1), p16.reshape(bsz, seq, -1)
            o = _dsa_attn(p16, p32, tq=128, tc=512)
            hp = _outproj([o.reshape(bsz * seq, -1)], [wo], hp, tm=tm)
            ko_p.append(p32[..., kc:kc + C_KV])
            vo_p.append(p32[..., vc:vc + C_KV])
            kio_p.append(p32[..., kic:kic + IDX_DIM])

            s32, s16 = _inproj(hs, g, w16, cos_s, sin_s, *ocols, tm=ms)
            s32, s16 = s32.reshape(nseq, n_new, -1), s16.reshape(nseq, n_new, -1)
            qi = s16[..., vc + C_KV:kic].reshape(nseq, n_new, IDX_HEADS, IDX_DIM)
            qi = jnp.swapaxes(qi, 1, 2).reshape(nseq, IDX_HEADS * n_new, IDX_DIM)
            wi = jnp.swapaxes(s32[..., wic:wic + IDX_HEADS], 1, 2).reshape(nseq, IDX_HEADS * n_new, 1)
            bias = _dec_select(page_table, qi, wi, cache_kidx_odd, e, _pad_rows(s16[..., kic:kic + IDX_DIM], PAGE_SIZE))
            qbd = _gqa_rows(s16[..., :C_Q].reshape(nseq, n_new, C_HEADS, HEAD_DIM))
            o = _dec_odd(page_table, qbd, bias, cache_k_odd, cache_v_odd, e,
                         _pad_rows(s16[..., kc:kc + C_KV], NEW_PAD), _pad_rows(s16[..., vc:vc + C_KV], NEW_PAD))
            hs = _outproj([o.reshape(ms, -1)], [wo], hs, tm=ms)
            ko_s.append(s32[..., kc:kc + C_KV])
            vo_s.append(s32[..., vc:vc + C_KV])
            kio_s.append(s32[..., kic:kic + IDX_DIM])

        last = layer == depth - 1
        ffn_w = (g_ffn[layer][None], bf(w_gate[layer]), bf(w_up[layer]), conv_w[layer], conv_b[layer][None],
                 bf(w_down[layer]), g_final[None])
        hp, tail = _ffn(hp, *ffn_w, seq_len=seq, tm=tm, tf=FFN_TF, final_norm=last)
        cs_p.append(tail.reshape(bsz, seq // tm, 8, ff)[:, -1, 8 - (CONV_W - 1):])
        hs, gout = _ffn_s(hs, state_conv[layer], *ffn_w, seq_len=n_new, tf=FFN_TF, final_norm=last)
        cs_s.append(gout.reshape(nseq, n_new, ff)[:, n_new - (CONV_W - 1):])

    def to_pages(rows):
        r = jnp.stack(rows, 1)
        b, nl, s, w = r.shape
        return r.reshape(b, nl, s // PAGE_SIZE, PAGE_SIZE, w).transpose(0, 2, 1, 3, 4)

    return (hp.reshape(bsz, seq, d), hs.reshape(nseq, n_new, d),
            to_pages(ke_p), to_pages(ve_p), to_pages(ko_p), to_pages(vo_p), to_pages(kio_p), jnp.stack(cs_p, 0),
            jnp.stack(ke_s, 1), jnp.stack(ve_s, 1), jnp.stack(ko_s, 1), jnp.stack(vo_s, 1), jnp.stack(kio_s, 1),
            jnp.stack(cs_s, 0))
```

```python
import functools
import math

import jax
import jax.numpy as jnp
from jax import lax
from jax.experimental import pallas as pl
from jax.experimental.pallas import tpu as pltpu

F32 = jnp.float32
BF16 = jnp.bfloat16

HEAD_DIM = 64
A_HEADS = 4
A_VDIM = 2 * HEAD_DIM
B_HEADS = 8
MOBA_BLOCK = 256
MOBA_TOPK = 3
C_HEADS = 16
C_KV_HEADS = 4
IDX_HEADS = 8
IDX_DIM = 64
DSA_TOPK = 256
CONV_W = 3
ROPE_THETA = 10000.0
EPS = 1e-6
PAGE_SIZE = 128
A_QK = A_HEADS * 2 * HEAD_DIM
A_V = A_HEADS * A_VDIM
B_W = B_HEADS * HEAD_DIM
C_Q = C_HEADS * HEAD_DIM
C_KV = C_KV_HEADS * HEAD_DIM
IDX_Q = IDX_HEADS * IDX_DIM
IDX_W_SCALE = IDX_Q ** -0.5
ATTN_SCALE = HEAD_DIM ** -0.5

LANES = 128
NEG = -1e30
VMEM_LIMIT = 56 * 1024 * 1024

EVEN_N = 2 * A_QK + A_V + 3 * B_W
ODD_N = C_Q + 2 * C_KV + IDX_Q + IDX_DIM + IDX_HEADS
ODD_NPAD = 2304
ODD_KI_COL = C_Q + 2 * C_KV + IDX_Q
PROJ_TN = 768


def _params(sem, vmem=VMEM_LIMIT):
    return pltpu.CompilerParams(dimension_semantics=sem, vmem_limit_bytes=vmem)


def _dot_nt(a, b):
    return lax.dot_general(a, b, (((1,), (1,)), ((), ())), preferred_element_type=F32)


def _dot(a, b):
    return jnp.dot(a, b, preferred_element_type=F32)


def _rms(x, g):
    return x * lax.rsqrt(jnp.mean(x * x, axis=-1, keepdims=True) + EPS) * g


def _inproj_kernel(x_ref, g_ref, w_ref, cos_ref, sin_ref, rmask_ref, s32_ref, s16_ref,
                   o32_ref, o16_ref, xn_ref):
    @pl.when(pl.program_id(1) == 0)
    def _():
        xn_ref[...] = _rms(x_ref[...], g_ref[...]).astype(BF16)

    y = _dot(xn_ref[...], w_ref[...])
    tn = y.shape[1]
    reps = tn // LANES
    cos = jnp.tile(cos_ref[...], (1, reps))
    sin = jnp.tile(sin_ref[...], (1, reps))
    lane = lax.broadcasted_iota(jnp.int32, y.shape, 1)
    first_half = (lane % HEAD_DIM) < (HEAD_DIM // 2)
    partner = jnp.where(first_half, pltpu.roll(y, tn - HEAD_DIM // 2, 1), pltpu.roll(y, HEAD_DIM // 2, 1))
    roped = y * cos + partner * sin
    out = jnp.where(rmask_ref[...] > 0.5, roped, y) * s32_ref[...]
    o32_ref[...] = out
    o16_ref[...] = (out * s16_ref[...]).astype(BF16)


def _inproj(x, g, w16, cos, sin, rmask, s32, s16, tm):
    m, d = x.shape
    n = w16.shape[1]
    tn = PROJ_TN
    tblocks = cos.shape[0] // tm
    return pl.pallas_call(
        _inproj_kernel,
        grid=(m // tm, n // tn),
        in_specs=[
            pl.BlockSpec((tm, d), lambda i, j: (i, 0)),
            pl.BlockSpec((1, d), lambda i, j: (0, 0)),
            pl.BlockSpec((d, tn), lambda i, j: (0, j)),
            pl.BlockSpec((tm, LANES), lambda i, j: (i % tblocks, 0)),
            pl.BlockSpec((tm, LANES), lambda i, j: (i % tblocks, 0)),
            pl.BlockSpec((1, tn), lambda i, j: (0, j)),
            pl.BlockSpec((1, tn), lambda i, j: (0, j)),
            pl.BlockSpec((1, tn), lambda i, j: (0, j)),
        ],
        out_specs=[
            pl.BlockSpec((tm, tn), lambda i, j: (i, j)),
            pl.BlockSpec((tm, tn), lambda i, j: (i, j)),
        ],
        out_shape=[jax.ShapeDtypeStruct((m, n), F32), jax.ShapeDtypeStruct((m, n), BF16)],
        scratch_shapes=[pltpu.VMEM((tm, d), BF16)],
        compiler_params=_params(("parallel", "arbitrary")),
        name="inproj",
    )(x, g, w16, cos, sin, rmask, s32, s16)


def _rope_tables(pos):
    half = HEAD_DIM // 2
    inv = ROPE_THETA ** (-jnp.arange(half, dtype=F32) * 2.0 / HEAD_DIM)
    ang = pos.astype(F32)[:, None] * inv[None, :]
    cos = jnp.tile(jnp.cos(ang), (1, LANES // half))
    sin = jnp.tile(jnp.concatenate([-jnp.sin(ang), jnp.sin(ang)], axis=1), (1, LANES // HEAD_DIM))
    return cos, sin


def _col_rows(n, rope_ranges, scale32, scale16):
    cols = jnp.arange(n)
    rmask = jnp.zeros((n,), F32)
    for lo, hi in rope_ranges:
        rmask = jnp.where((cols >= lo) & (cols < hi), 1.0, rmask)
    s32 = jnp.ones((n,), F32)
    for lo, hi, v in scale32:
        s32 = jnp.where((cols >= lo) & (cols < hi), v, s32)
    s16 = jnp.ones((n,), F32)
    for lo, hi, v in scale16:
        s16 = jnp.where((cols >= lo) & (cols < hi), v, s16)
    return rmask[None], s32[None], s16[None]


def _even_cols():
    o = A_QK + A_QK + A_V
    return _col_rows(EVEN_N, [(0, 2 * A_QK), (o, o + 2 * B_W)], [],
                     [(0, A_QK, ATTN_SCALE), (o, o + B_W, ATTN_SCALE)])


def _odd_cols():
    return _col_rows(ODD_NPAD, [(0, C_Q + C_KV), (C_Q + 2 * C_KV, ODD_KI_COL + IDX_DIM)],
                     [(ODD_KI_COL + IDX_DIM, ODD_N, IDX_W_SCALE)], [(0, C_Q, ATTN_SCALE)])


def _outproj_kernel(*refs, n_in):
    xs, ws, h_ref, o_ref = refs[:n_in], refs[n_in:2 * n_in], refs[2 * n_in], refs[2 * n_in + 1]
    acc = h_ref[...]
    for x_ref, w_ref in zip(xs, ws):
        acc = acc + _dot(x_ref[...], w_ref[...])
    o_ref[...] = acc


def _outproj(xs, ws, h, tm):
    m, d = h.shape
    n_in = len(xs)
    in_specs = [pl.BlockSpec((tm, x.shape[1]), lambda i: (i, 0)) for x in xs]
    in_specs += [pl.BlockSpec(w.shape, lambda i: (0, 0)) for w in ws]
    in_specs += [pl.BlockSpec((tm, d), lambda i: (i, 0))]
    return pl.pallas_call(
        functools.partial(_outproj_kernel, n_in=n_in),
        grid=(m // tm,),
        in_specs=in_specs,
        out_specs=pl.BlockSpec((tm, d), lambda i: (i, 0)),
        out_shape=jax.ShapeDtypeStruct((m, d), F32),
        compiler_params=_params(("parallel",)),
        name="outproj",
    )(*xs, *ws, h)


FFN_HALO = 16


def _ffn_kernel(h_ref, halo_ref, g_ref, wg_ref, wu_ref, cw_ref, cb_ref, wd_ref, gf_ref,
                o_ref, tail_ref, xn_ref, gext_ref, acc_ref, *, tiles_per_seq, final_norm):
    i, f = pl.program_id(0), pl.program_id(1)
    tm = h_ref.shape[0]

    @pl.when(f == 0)
    def _():
        xn_ref[0:FFN_HALO, :] = _rms(halo_ref[...], g_ref[...]).astype(BF16)
        xn_ref[FFN_HALO:, :] = _rms(h_ref[...], g_ref[...]).astype(BF16)
        acc_ref[...] = jnp.zeros_like(acc_ref)

    xn = xn_ref[...]
    gext = _dot(xn, wg_ref[...])
    first = (i % tiles_per_seq) == 0
    row = lax.broadcasted_iota(jnp.int32, gext.shape, 0)
    gext_ref[...] = jnp.where(jnp.logical_and(first, row < FFN_HALO), 0.0, gext)
    g = gext_ref[FFN_HALO:, :]
    p1 = gext_ref[pl.ds(FFN_HALO - 1, tm), :]
    p2 = gext_ref[pl.ds(FFN_HALO - 2, tm), :]
    cw = cw_ref[...]
    c = cb_ref[...] + p2 * cw[0:1] + p1 * cw[1:2] + g * cw[2:3]
    u = _dot(xn[FFN_HALO:], wu_ref[...])
    a = (c * jax.nn.sigmoid(c) * u).astype(BF16)
    acc_ref[...] += _dot(a, wd_ref[...])
    tail_ref[0] = g[tm - 8:, :]

    @pl.when(f == pl.num_programs(1) - 1)
    def _():
        out = h_ref[...] + acc_ref[...]
        if final_norm:
            out = _rms(out, gf_ref[...])
        o_ref[...] = out


def _ffn(h, g, wg, wu, cw, cb, wd, gf, seq_len, tm, tf, final_norm):
    m, d = h.shape
    ff = wg.shape[1]
    nt = m // tm
    hb = tm // FFN_HALO
    out, tail = pl.pallas_call(
        functools.partial(_ffn_kernel, tiles_per_seq=seq_len // tm, final_norm=final_norm),
        grid=(nt, ff // tf),
        in_specs=[
            pl.BlockSpec((tm, d), lambda i, f: (i, 0)),
            pl.BlockSpec((FFN_HALO, d), lambda i, f: (jnp.maximum(i * hb - 1, 0), 0)),
            pl.BlockSpec((1, d), lambda i, f: (0, 0)),
            pl.BlockSpec((d, tf), lambda i, f: (0, f)),
            pl.BlockSpec((d, tf), lambda i, f: (0, f)),
            pl.BlockSpec((CONV_W, tf), lambda i, f: (0, f)),
            pl.BlockSpec((1, tf), lambda i, f: (0, f)),
            pl.BlockSpec((tf, d), lambda i, f: (f, 0)),
            pl.BlockSpec((1, d), lambda i, f: (0, 0)),
        ],
        out_specs=[
            pl.BlockSpec((tm, d), lambda i, f: (i, 0)),
            pl.BlockSpec((1, 8, tf), lambda i, f: (i, 0, f)),
        ],
        out_shape=[jax.ShapeDtypeStruct((m, d), F32), jax.ShapeDtypeStruct((nt, 8, ff), F32)],
        scratch_shapes=[pltpu.VMEM((FFN_HALO + tm, d), BF16), pltpu.VMEM((FFN_HALO + tm, tf), F32),
                        pltpu.VMEM((tm, d), F32)],
        compiler_params=_params(("parallel", "arbitrary")),
        name="ffn",
    )(h, h, g, wg, wu, cw, cb, wd, gf)
    return out, tail


def _ffn_s_kernel(h_ref, b1_ref, b2_ref, g_ref, wg_ref, wu_ref, cw_ref, cb_ref, wd_ref, gf_ref,
                  o_ref, gout_ref, xn_ref, acc_ref, *, seq_len, final_norm):
    f = pl.program_id(0)

    @pl.when(f == 0)
    def _():
        xn_ref[...] = _rms(h_ref[...], g_ref[...]).astype(BF16)
        acc_ref[...] = jnp.zeros_like(acc_ref)

    xn = xn_ref[...]
    g = _dot(xn, wg_ref[...])
    t = lax.broadcasted_iota(jnp.int32, g.shape, 0) % seq_len
    p1 = jnp.where(t < 1, b1_ref[...], pltpu.roll(g, 1, 0))
    p2 = jnp.where(t < 2, b2_ref[...], pltpu.roll(g, 2, 0))
    cw = cw_ref[...]
    c = cb_ref[...] + p2 * cw[0:1] + p1 * cw[1:2] + g * cw[2:3]
    u = _dot(xn, wu_ref[...])
    a = (c * jax.nn.sigmoid(c) * u).astype(BF16)
    acc_ref[...] += _dot(a, wd_ref[...])
    gout_ref[...] = g

    @pl.when(f == pl.num_programs(0) - 1)
    def _():
        out = h_ref[...] + acc_ref[...]
        if final_norm:
            out = _rms(out, gf_ref[...])
        o_ref[...] = out


def _ffn_s(h, buf, g, wg, wu, cw, cb, wd, gf, seq_len, tf, final_norm):
    m, d = h.shape
    ff = wg.shape[1]
    nseq = m // seq_len
    zeros = jnp.zeros((nseq, seq_len - 2, ff), F32)
    b1 = jnp.concatenate([buf[:, 1:2], jnp.zeros((nseq, 1, ff), F32), zeros], axis=1).reshape(m, ff)
    b2 = jnp.concatenate([buf[:, 0:1], buf[:, 1:2], zeros], axis=1).reshape(m, ff)
    out, gout = pl.pallas_call(
        functools.partial(_ffn_s_kernel, seq_len=seq_len, final_norm=final_norm),
        grid=(ff // tf,),
        in_specs=[
            pl.BlockSpec((m, d), lambda f: (0, 0)),
            pl.BlockSpec((m, tf), lambda f: (0, f)),
            pl.BlockSpec((m, tf), lambda f: (0, f)),
            pl.BlockSpec((1, d), lambda f: (0, 0)),
            pl.BlockSpec((d, tf), lambda f: (0, f)),
            pl.BlockSpec((d, tf), lambda f: (0, f)),
            pl.BlockSpec((CONV_W, tf), lambda f: (0, f)),
            pl.BlockSpec((1, tf), lambda f: (0, f)),
            pl.BlockSpec((tf, d), lambda f: (f, 0)),
            pl.BlockSpec((1, d), lambda f: (0, 0)),
        ],
        out_specs=[
            pl.BlockSpec((m, d), lambda f: (0, 0)),
            pl.BlockSpec((m, tf), lambda f: (0, f)),
        ],
        out_shape=[jax.ShapeDtypeStruct((m, d), F32), jax.ShapeDtypeStruct((m, ff), F32)],
        scratch_shapes=[pltpu.VMEM((m, d), BF16), pltpu.VMEM((m, d), F32)],
        compiler_params=_params(("arbitrary",)),
        name="ffn_sample",
    )(h, b1, b2, g, wg, wu, cw, cb, wd, gf)
    return out, gout


SLAB = 2 * HEAD_DIM


def _heads_t(q, slot):
    b, l, nh, dd = q.shape
    keep = jnp.asarray(slot)[:, None] == jnp.arange(2)[None, :]
    qt = jnp.transpose(q, (0, 2, 3, 1))
    out = jnp.where(keep[None, :, :, None, None], qt[:, :, None, :, :], jnp.zeros((), q.dtype))
    return out.reshape(b, nh * SLAB, l)


def _flash_update_t(st, vt, m_ref, l_ref, acc_ref, idx):
    m_prev = m_ref[idx]
    m_new = jnp.maximum(m_prev, jnp.max(st, axis=0, keepdims=True))
    alpha = jnp.exp(m_prev - m_new)
    p = jnp.exp(st - m_new)
    l_ref[idx] = alpha * l_ref[idx] + jnp.sum(p, axis=0, keepdims=True)
    acc_ref[idx] = alpha * acc_ref[idx] + _dot(vt, p.astype(BF16))
    m_ref[idx] = m_new


def _flash_init(m_ref, l_ref, acc_ref):
    m_ref[...] = jnp.full_like(m_ref, NEG)
    l_ref[...] = jnp.zeros_like(l_ref)
    acc_ref[...] = jnp.zeros_like(acc_ref)


def _diff_kernel(lam_ref, qt_ref, k_ref, vt_ref, g_ref, o_ref, m_ref, l_ref, acc_ref, *, tq, tk, out_scale):
    i, j = pl.program_id(1), pl.program_id(2)
    jlast = ((i + 1) * tq - 1) // tk

    @pl.when(j == 0)
    def _():
        _flash_init(m_ref, l_ref, acc_ref)

    def step(masked):
        if masked:
            kpos = j * tk + lax.broadcasted_iota(jnp.int32, (tk, 2 * tq), 0)
            qpos = i * tq + lax.broadcasted_iota(jnp.int32, (tk, 2 * tq), 1) % tq
            ok = kpos <= qpos
        for h in range(A_HEADS):
            q2 = jnp.concatenate([qt_ref[0, m * SLAB:(m + 1) * SLAB, :] for m in (2 * h, 2 * h + 1)], axis=1)
            s = _dot(k_ref[0, :, h * SLAB:(h + 1) * SLAB], q2)
            if masked:
                s = jnp.where(ok, s, NEG)
            _flash_update_t(s, vt_ref[0, h * A_VDIM:(h + 1) * A_VDIM, :], m_ref, l_ref, acc_ref, h)

    crosses = (j + 1) * tk - 1 > i * tq

    @pl.when(jnp.logical_and(j <= jlast, crosses))
    def _():
        step(True)

    @pl.when(jnp.logical_and(j <= jlast, jnp.logical_not(crosses)))
    def _():
        step(False)

    @pl.when(j == jlast)
    def _():
        lam = lam_ref[0]
        for h in range(A_HEADS):
            r = acc_ref[h] / l_ref[h]
            o = r[:, :tq] - lam * r[:, tq:]
            o = o * lax.rsqrt(jnp.mean(o * o, axis=0, keepdims=True) + EPS) * g_ref[...] * out_scale
            o_ref[0, h * A_VDIM:(h + 1) * A_VDIM, :] = o.astype(BF16)


def _diff_attn(lam, qt, qkv16, vt, subln_col, out_scale, tq, tk):
    b, l, _ = qkv16.shape
    nq, nk = l // tq, l // tk
    jmax = lambda i, j: jnp.minimum(j, ((i + 1) * tq - 1) // tk)
    return pl.pallas_call(
        functools.partial(_diff_kernel, tq=tq, tk=tk, out_scale=out_scale),
        grid_spec=pltpu.PrefetchScalarGridSpec(
            num_scalar_prefetch=1,
            grid=(b, nq, nk),
            in_specs=[
                pl.BlockSpec((1, 2 * A_HEADS * SLAB, tq), lambda bi, i, j, lam_ref: (bi, 0, i)),
                pl.BlockSpec((1, tk, A_QK), lambda bi, i, j, lam_ref: (bi, jmax(i, j), 1)),
                pl.BlockSpec((1, A_V, tk), lambda bi, i, j, lam_ref: (bi, 0, jmax(i, j))),
                pl.BlockSpec((A_VDIM, 1), lambda bi, i, j, lam_ref: (0, 0)),
            ],
            out_specs=pl.BlockSpec((1, A_V, tq), lambda bi, i, j, lam_ref: (bi, 0, i)),
            scratch_shapes=[pltpu.VMEM((A_HEADS, 1, 2 * tq), F32), pltpu.VMEM((A_HEADS, 1, 2 * tq), F32),
                            pltpu.VMEM((A_HEADS, A_VDIM, 2 * tq), F32)],
        ),
        out_shape=jax.ShapeDtypeStruct((b, A_V, l), BF16),
        compiler_params=_params(("parallel", "parallel", "arbitrary")),
        name="diff_attn",
    )(lam, qt, qkv16, vt, subln_col)


def _blockmean_kernel(k_ref, o_ref):
    nb = o_ref.shape[1]
    for n in range(nb):
        o_ref[0, n:n + 1, :] = jnp.mean(k_ref[0, n * MOBA_BLOCK:(n + 1) * MOBA_BLOCK, :], axis=0, keepdims=True)


def _blockmean(x32, col_block, width, nb_step):
    b, l, _ = x32.shape
    nb = l // MOBA_BLOCK
    return pl.pallas_call(
        _blockmean_kernel,
        grid=(b, nb // nb_step),
        in_specs=[pl.BlockSpec((1, nb_step * MOBA_BLOCK, width), lambda bi, i: (bi, i, col_block))],
        out_specs=pl.BlockSpec((1, nb_step, width), lambda bi, i: (bi, i, 0)),
        out_shape=jax.ShapeDtypeStruct((b, nb, width), F32),
        compiler_params=_params(("parallel", "parallel")),
        name="blockmean",
    )(x32)


def _top3_select(gate, n_valid, axis=1):
    nb = gate.shape[axis]
    idx = lax.broadcasted_iota(jnp.int32, gate.shape, axis)
    sel = jnp.zeros(gate.shape, F32)
    for r in range(MOBA_TOPK):
        mx = jnp.max(gate, axis=axis, keepdims=True)
        first = jnp.min(jnp.where(gate == mx, idx, nb), axis=axis, keepdims=True)
        pick = idx == first
        sel = jnp.where(jnp.logical_and(pick, r < n_valid), 1.0, sel)
        gate = jnp.where(pick, -jnp.inf, gate)
    return sel


def _dot_nt_f32(a, b):
    return lax.dot_general(a, b, (((1,), (1,)), ((), ())), preferred_element_type=F32,
                           precision=lax.Precision.HIGHEST)


def _dot_f32(a, b):
    return jnp.dot(a, b, preferred_element_type=F32, precision=lax.Precision.HIGHEST)


def _moba_kernel(qt_ref, q32t_ref, k_ref, vt_ref, km_ref, o_ref, sel_ref, m_ref, l_ref, acc_ref, *, sub):
    i, j = pl.program_id(1), pl.program_id(2)
    tq = MOBA_BLOCK
    nb = km_ref.shape[1]
    jlast = i // sub

    @pl.when(j == 0)
    def _():
        _flash_init(m_ref, l_ref, acc_ref)
        blk = lax.broadcasted_iota(jnp.int32, (nb, tq), 0)
        for h in range(B_HEADS):
            rows = slice(h * HEAD_DIM, (h + 1) * HEAD_DIM)
            gate = _dot_f32(km_ref[0, :, rows], q32t_ref[0, rows, :])
            sel_ref[h] = _top3_select(jnp.where(blk < i, gate, -jnp.inf), i, axis=0)

    def sub_block(r, own):
        n = j * sub + r
        keys = slice(r * MOBA_BLOCK, (r + 1) * MOBA_BLOCK)
        if own:
            ok = (lax.broadcasted_iota(jnp.int32, (MOBA_BLOCK, 2 * tq), 0)
                  <= lax.broadcasted_iota(jnp.int32, (MOBA_BLOCK, 2 * tq), 1) % tq)
        for p in range(B_HEADS // 2):
            q2 = jnp.concatenate([qt_ref[0, h * SLAB:(h + 1) * SLAB, :] for h in (2 * p, 2 * p + 1)], axis=1)
            s = _dot(k_ref[0, keys, p * SLAB:(p + 1) * SLAB], q2)
            if not own:
                ok = jnp.concatenate([sel_ref[h, pl.ds(n, 1), :] for h in (2 * p, 2 * p + 1)], axis=1) > 0.5
            s = jnp.where(ok, s, NEG)
            _flash_update_t(s, vt_ref[0, p * SLAB:(p + 1) * SLAB, keys], m_ref, l_ref, acc_ref, p)

    for r in range(sub):
        n = j * sub + r

        @pl.when(jnp.logical_and(j <= jlast, n < i))
        def _():
            sub_block(r, False)

        @pl.when(n == i)
        def _():
            sub_block(r, True)

    @pl.when(j == jlast)
    def _():
        for h in range(B_HEADS):
            rows = slice((h % 2) * HEAD_DIM, (h % 2 + 1) * HEAD_DIM)
            cols = slice((h % 2) * tq, (h % 2 + 1) * tq)
            o_ref[0, h * HEAD_DIM:(h + 1) * HEAD_DIM, :] = (acc_ref[h // 2, rows, cols] / l_ref[h // 2, :, cols]).astype(BF16)


def _moba_attn(qt, q32t, qkv16, vt, kmean, sub):
    b, l, _ = qkv16.shape
    nb = l // MOBA_BLOCK
    tk = sub * MOBA_BLOCK
    kcol = (2 * A_QK + A_V + B_W) // B_W
    jmax = lambda i, j: jnp.minimum(j, i // sub)
    return pl.pallas_call(
        functools.partial(_moba_kernel, sub=sub),
        grid=(b, nb, l // tk),
        in_specs=[
            pl.BlockSpec((1, B_HEADS * SLAB, MOBA_BLOCK), lambda bi, i, j: (bi, 0, i)),
            pl.BlockSpec((1, B_W, MOBA_BLOCK), lambda bi, i, j: (bi, 0, i)),
            pl.BlockSpec((1, tk, B_W), lambda bi, i, j: (bi, jmax(i, j), kcol)),
            pl.BlockSpec((1, B_W, tk), lambda bi, i, j: (bi, 0, jmax(i, j))),
            pl.BlockSpec((1, nb, B_W), lambda bi, i, j: (bi, 0, 0)),
        ],
        out_specs=pl.BlockSpec((1, B_W, MOBA_BLOCK), lambda bi, i, j: (bi, 0, i)),
        out_shape=jax.ShapeDtypeStruct((b, B_W, l), BF16),
        scratch_shapes=[pltpu.VMEM((B_HEADS, nb, MOBA_BLOCK), F32),
                        pltpu.VMEM((B_HEADS // 2, 1, 2 * MOBA_BLOCK), F32),
                        pltpu.VMEM((B_HEADS // 2, 1, 2 * MOBA_BLOCK), F32),
                        pltpu.VMEM((B_HEADS // 2, SLAB, 2 * MOBA_BLOCK), F32)],
        compiler_params=_params(("parallel", "parallel", "arbitrary")),
        name="moba_attn",
    )(qt, q32t, qkv16, vt, kmean)


INT_MIN = -2 ** 31


def _order_key(score):
    bits = pltpu.bitcast(jnp.where(score == 0.0, 0.0, score), jnp.int32)
    return bits ^ ((bits >> 31) & 0x7FFFFFFF)


SUBLANES = 8


def _fold(x, axis):
    if axis == 1:
        part = x[:, 0:LANES]
        for s in range(1, x.shape[1] // LANES):
            part = part + x[:, s * LANES:(s + 1) * LANES]
        return part
    return jnp.sum(x.reshape(x.shape[0] // SUBLANES, SUBLANES, x.shape[1]), axis=0)


def _count(key_ref, nch, preds, axis):
    shape = key_ref.shape[1:]
    part = (shape[0], LANES) if axis == 1 else (SUBLANES, shape[1])

    def body(c, accs):
        blk = key_ref[c]
        return tuple(a + _fold(jnp.where(p(blk), 1.0, 0.0), axis) for a, p in zip(accs, preds))

    accs = lax.fori_loop(0, nch, body, tuple(jnp.zeros(part, F32) for _ in preds))
    return tuple(jnp.sum(a, axis=axis, keepdims=True) for a in accs)


def _kth_largest_key(key_ref, nch, kk, axis):
    def cond(carry):
        it, _, _, n_ge = carry
        return jnp.logical_and(it < 32, jnp.max(n_ge - kk) > 0.5)

    def search(carry):
        it, t_u, bit, n_ge = carry
        cand_u = t_u | bit
        cand = cand_u ^ INT_MIN
        cnt, = _count(key_ref, nch, (lambda blk: blk >= cand,), axis)
        keep = cnt >= kk
        return it + 1, jnp.where(keep, cand_u, t_u), lax.shift_right_logical(bit, 1), jnp.where(keep, cnt, n_ge)

    n_all, = _count(key_ref, nch, (lambda blk: blk > INT_MIN,), axis)
    _, t_u, _, n_ge = lax.while_loop(
        cond, search, (jnp.int32(0), jnp.zeros(kk.shape, jnp.int32), jnp.full(kk.shape, INT_MIN, jnp.int32), n_all))
    return jnp.maximum(t_u ^ INT_MIN, INT_MIN + 1), n_ge


def _topk_bias(key_ref, out_ref, tri_ref, nch, kk, axis):
    tc = key_ref.shape[1 + axis]
    t, n_ge = _kth_largest_key(key_ref, nch, kk, axis)
    ties = jnp.max(n_ge - kk) > 0.5

    def put(c, take):
        bias = jnp.where(take, 0.0, NEG)
        out_ref[c] = pltpu.bitcast(bias, jnp.int32) if out_ref.dtype == jnp.int32 else bias

    @pl.when(jnp.logical_not(ties))
    def _():
        def body(c, _):
            put(c, key_ref[c] >= t)
            return 0
        lax.fori_loop(0, nch, body, 0)

    @pl.when(ties)
    def _():
        n_gt, = _count(key_ref, nch, (lambda blk: blk > t,), axis)
        need = kk - n_gt
        r = lax.broadcasted_iota(jnp.int32, (tc, tc), 0)
        cidx = lax.broadcasted_iota(jnp.int32, (tc, tc), 1)
        tri_ref[...] = jnp.where((r <= cidx) if axis == 1 else (cidx <= r), 1.0, 0.0).astype(BF16)

        def body(c, seen):
            blk = key_ref[c]
            eqm = blk == t
            eq16 = jnp.where(eqm, 1.0, 0.0).astype(BF16)
            rank = seen + (_dot(eq16, tri_ref[...]) if axis == 1 else _dot(tri_ref[...], eq16))
            put(c, jnp.logical_or(blk > t, jnp.logical_and(eqm, rank <= need)))
            return seen + jnp.sum(jnp.where(eqm, 1.0, 0.0), axis=axis, keepdims=True)
        lax.fori_loop(0, nch, body, jnp.zeros(kk.shape, F32))


def _dsa_kernel(qt_ref, qit_ref, wit_ref, k_ref, vt_ref, ki_ref, o_ref,
                key_ref, tri_ref, m_ref, l_ref, acc_ref, *, tq, tc, topk):
    i = pl.program_id(1)
    nch = ((i + 1) * tq - 1) // tc + 1
    qpos = i * tq + lax.broadcasted_iota(jnp.int32, (1, tq), 1)
    kk = jnp.minimum(topk, qpos + 1).astype(F32)

    def score_chunk(c, _):
        r0 = pl.multiple_of(c * tc, tc)
        kic = ki_ref[0, pl.ds(r0, tc), :]
        score = jnp.zeros((tc, tq), F32)
        for h in range(IDX_HEADS):
            lg = _dot(kic, qit_ref[0, h * SLAB:(h + 1) * SLAB, :])
            score = score + wit_ref[0, h:h + 1, :] * jnp.maximum(lg, 0.0)
        kpos = r0 + lax.broadcasted_iota(jnp.int32, (tc, tq), 0)
        key_ref[c] = jnp.where(kpos <= qpos, _order_key(score), INT_MIN)
        return 0

    lax.fori_loop(0, nch, score_chunk, 0)
    _topk_bias(key_ref, key_ref, tri_ref, nch, kk, axis=0)
    _flash_init(m_ref, l_ref, acc_ref)
    group = C_HEADS // C_KV_HEADS

    def attend(c, _):
        r0 = pl.multiple_of(c * tc, tc)
        bias = pltpu.bitcast(key_ref[c], F32)
        bias = jnp.concatenate([bias] * group, axis=1)
        for g in range(C_KV_HEADS):
            qg = jnp.concatenate([qt_ref[0, h * SLAB:(h + 1) * SLAB, :] for h in range(g * group, (g + 1) * group)], axis=1)
            s = _dot(k_ref[0, pl.ds(r0, tc), (g // 2) * SLAB:(g // 2 + 1) * SLAB], qg) + bias
            _flash_update_t(s, vt_ref[0, c, g * HEAD_DIM:(g + 1) * HEAD_DIM, :], m_ref, l_ref, acc_ref, g)
        return 0

    lax.fori_loop(0, nch, attend, 0)
    for h in range(C_HEADS):
        g, hh = h // group, h % group
        cols = slice(hh * tq, (hh + 1) * tq)
        o_ref[0, h * HEAD_DIM:(h + 1) * HEAD_DIM, :] = (acc_ref[g, :, cols] / l_ref[g, :, cols]).astype(BF16)


def _dsa_attn(qt, qit, wit, qkv16, vt4, tq, tc):
    b, l, _ = qkv16.shape
    topk = min(DSA_TOPK, l // 4)
    nc = l // tc
    gq = (C_HEADS // C_KV_HEADS) * tq
    return pl.pallas_call(
        functools.partial(_dsa_kernel, tq=tq, tc=tc, topk=topk),
        grid=(b, l // tq),
        in_specs=[
            pl.BlockSpec((1, C_HEADS * SLAB, tq), lambda bi, i: (bi, 0, i)),
            pl.BlockSpec((1, IDX_HEADS * SLAB, tq), lambda bi, i: (bi, 0, i)),
            pl.BlockSpec((1, IDX_HEADS, tq), lambda bi, i: (bi, 0, i)),
            pl.BlockSpec((1, l, C_KV), lambda bi, i: (bi, 0, C_Q // C_KV)),
            pl.BlockSpec((1, nc, C_KV, tc), lambda bi, i: (bi, 0, 0, 0)),
            pl.BlockSpec((1, l, SLAB), lambda bi, i: (bi, 0, ODD_KI_COL // SLAB)),
        ],
        out_specs=pl.BlockSpec((1, C_Q, tq), lambda bi, i: (bi, 0, i)),
        out_shape=jax.ShapeDtypeStruct((b, C_Q, l), BF16),
        scratch_shapes=[pltpu.VMEM((nc, tc, tq), jnp.int32), pltpu.VMEM((tc, tc), BF16),
                        pltpu.VMEM((C_KV_HEADS, 1, gq), F32), pltpu.VMEM((C_KV_HEADS, 1, gq), F32),
                        pltpu.VMEM((C_KV_HEADS, HEAD_DIM, gq), F32)],
        compiler_params=_params(("parallel", "arbitrary")),
        name="dsa_attn",
    )(qt, qit, wit, qkv16, vt4, qkv16)


NEW_PAD = 16
EVEN_PAGES = MOBA_BLOCK // PAGE_SIZE


def _fold_heads(x, n_groups, n_new):
    return jnp.sum(x.reshape(n_groups, n_new, x.shape[1]), axis=0)


def _dec_even_kernel(pt_ref, lam_ref, qbd_ref, *refs, n_new, n_blocks, out_scale):
    kpages = refs[:EVEN_PAGES]
    vpages = refs[EVEN_PAGES:2 * EVEN_PAGES]
    knew_ref, vnew_ref, g_ref, o_ref, ma_ref, la_ref, acca_ref, mb_ref, lb_ref, accb_ref, km_ref = refs[2 * EVEN_PAGES:]
    p = pl.program_id(1)
    ra = 8 * n_new
    qbd = qbd_ref[0]

    @pl.when(p == 0)
    def _():
        ma_ref[...] = jnp.full_like(ma_ref, NEG)
        la_ref[...] = jnp.zeros_like(la_ref)
        acca_ref[...] = jnp.zeros_like(acca_ref)
        km_ref[...] = jnp.zeros_like(km_ref)

    def attend(k16, v16, mask, blk):
        s = _dot_nt(qbd, k16)
        if mask is not None:
            s = jnp.where(mask, s, NEG)
        sa, sb = s[:ra], s[ra:]
        m_prev = ma_ref[...]
        m_new = jnp.maximum(m_prev, jnp.max(sa, axis=1, keepdims=True))
        alpha = jnp.exp(m_prev - m_new)
        pa = jnp.exp(sa - m_new)
        la_ref[...] = alpha * la_ref[...] + jnp.sum(pa, axis=1, keepdims=True)
        acca_ref[...] = alpha * acca_ref[...] + _dot(pa.astype(BF16), v16[:, :A_V])
        ma_ref[...] = m_new
        mb = jnp.max(sb, axis=1, keepdims=True)
        pb = jnp.exp(sb - mb)
        mb_ref[blk] = mb
        lb_ref[blk] = jnp.sum(pb, axis=1, keepdims=True)
        accb_ref[blk] = _dot(pb.astype(BF16), v16[:, A_V:])

    @pl.when(p < n_blocks)
    def _():
        k32 = jnp.concatenate([r[...] for r in kpages], axis=0)
        v32 = jnp.concatenate([r[...] for r in vpages], axis=0)
        km_ref[pl.ds(p, 1), :] = jnp.mean(k32[:, A_QK:], axis=0, keepdims=True)
        attend(k32.astype(BF16), v32.astype(BF16), None, p)

    @pl.when(p == n_blocks)
    def _():
        rows = 16 * n_new
        t = lax.broadcasted_iota(jnp.int32, (rows, NEW_PAD), 0) % n_new
        j = lax.broadcasted_iota(jnp.int32, (rows, NEW_PAD), 1)
        attend(knew_ref[0], vnew_ref[0], j <= t, n_blocks)

        lam = lam_ref[0]
        r = acca_ref[...] / la_ref[...]
        m_idx = lax.broadcasted_iota(jnp.int32, r.shape, 0) // n_new
        col = lax.broadcasted_iota(jnp.int32, r.shape, 1)
        coef = jnp.where(m_idx % 2 == 0, 1.0, -lam)
        oa = _fold_heads(jnp.where(col // A_VDIM == m_idx // 2, r * coef, 0.0), 8, n_new)
        for h in range(A_HEADS):
            cs = slice(h * A_VDIM, (h + 1) * A_VDIM)
            o_ref[0, :, cs] = (_rms(oa[:, cs], g_ref[...]) * out_scale).astype(BF16)

        nbp = km_ref.shape[0]
        gate = _dot_nt_f32(qbd[ra:, A_QK:].astype(F32), km_ref[...])
        blk = lax.broadcasted_iota(jnp.int32, gate.shape, 1)
        sel = _top3_select(jnp.where(blk < n_blocks, gate, -jnp.inf), n_blocks)
        m_run, l_run, acc = mb_ref[n_blocks], lb_ref[n_blocks], accb_ref[n_blocks]
        for n in range(n_blocks):
            chosen = sel[:, n:n + 1] > 0.5
            m_n = mb_ref[n]
            m_new = jnp.where(chosen, jnp.maximum(m_run, m_n), m_run)
            a_old = jnp.exp(m_run - m_new)
            a_n = jnp.where(chosen, jnp.exp(m_n - m_new), 0.0)
            l_run = a_old * l_run + a_n * lb_ref[n]
            acc = a_old * acc + a_n * accb_ref[n]
            m_run = m_new
        r = acc / l_run
        h_idx = lax.broadcasted_iota(jnp.int32, r.shape, 0) // n_new
        col = lax.broadcasted_iota(jnp.int32, r.shape, 1)
        ob = _fold_heads(jnp.where(col // HEAD_DIM == h_idx, r, 0.0), 8, n_new)
        o_ref[0, :, A_V:] = ob.astype(BF16)


def _block_diag_rows(q, n_maps, width):
    s, t, c = q.shape
    keep = (jnp.arange(c)[None, :] // width) == jnp.arange(n_maps)[:, None]
    return jnp.where(keep[None, :, None, :], q[:, None, :, :], 0).reshape(s, n_maps * t, c).astype(q.dtype)


def _dec_even(page_table, lam, qbd, cache_k, cache_v, layer, knew, vnew, subln_g, out_scale):
    nseq, n_pages = page_table.shape
    n_new = qbd.shape[1] // 16
    n_blocks = n_pages // EVEN_PAGES
    width = cache_k.shape[-1]

    def page_spec(r):
        return pl.BlockSpec((None, None, PAGE_SIZE, width),
                            lambda s, p, pt, lm: (pt[s, jnp.minimum(p, n_blocks - 1) * EVEN_PAGES + r], layer, 0, 0))

    seq_spec = lambda shape: pl.BlockSpec((1,) + shape, lambda s, p, pt, lm: (s, 0, 0))
    ra = 8 * n_new
    nbp = -(-(n_blocks + 1) // 8) * 8
    return pl.pallas_call(
        functools.partial(_dec_even_kernel, n_new=n_new, n_blocks=n_blocks, out_scale=out_scale),
        grid_spec=pltpu.PrefetchScalarGridSpec(
            num_scalar_prefetch=2,
            grid=(nseq, n_blocks + 1),
            in_specs=[seq_spec((16 * n_new, width))]
            + [page_spec(r) for r in range(EVEN_PAGES)] + [page_spec(r) for r in range(EVEN_PAGES)]
            + [seq_spec((NEW_PAD, width)), seq_spec((NEW_PAD, width)),
               pl.BlockSpec((1, A_VDIM), lambda s, p, pt, lm: (0, 0))],
            out_specs=seq_spec((n_new, A_V + B_W)),
            scratch_shapes=[pltpu.VMEM((ra, 1), F32), pltpu.VMEM((ra, 1), F32), pltpu.VMEM((ra, A_V), F32),
                            pltpu.VMEM((n_blocks + 1, ra, 1), F32), pltpu.VMEM((n_blocks + 1, ra, 1), F32),
                            pltpu.VMEM((n_blocks + 1, ra, B_W), F32), pltpu.VMEM((nbp, B_W), F32)],
        ),
        out_shape=jax.ShapeDtypeStruct((nseq, n_new, A_V + B_W), BF16),
        compiler_params=_params(("parallel", "arbitrary")),
        name="dec_even",
    )(page_table, lam, qbd, *([cache_k] * EVEN_PAGES), *([cache_v] * EVEN_PAGES), knew, vnew, subln_g)


ODD_PAGES = 8
ODD_CHUNK = ODD_PAGES * PAGE_SIZE


def _dec_select_kernel(pt_ref, qi_ref, wi_ref, *refs, n_new, n_chunks, topk):
    pages = refs[:ODD_PAGES]
    kinew_ref, bias_ref, key_ref, tri_ref = refs[ODD_PAGES:]
    p = pl.program_id(1)
    qi = qi_ref[0]
    wi = wi_ref[0]

    def scores(ki16):
        lg = _dot_nt(qi, ki16)
        return _fold_heads(wi * jnp.maximum(lg, 0.0), IDX_HEADS, n_new)

    @pl.when(p < n_chunks)
    def _():
        for r in range(ODD_PAGES):
            key_ref[p, :, r * PAGE_SIZE:(r + 1) * PAGE_SIZE] = _order_key(scores(pages[r][...].astype(BF16)))

    @pl.when(p == n_chunks)
    def _():
        sc = scores(kinew_ref[0])
        t = lax.broadcasted_iota(jnp.int32, sc.shape, 0)
        j = lax.broadcasted_iota(jnp.int32, sc.shape, 1)
        key_ref[n_chunks] = jnp.full((n_new, ODD_CHUNK), INT_MIN, jnp.int32)
        key_ref[n_chunks, :, 0:PAGE_SIZE] = jnp.where(j <= t, _order_key(sc), INT_MIN)
        qpos = n_chunks * ODD_CHUNK + lax.broadcasted_iota(jnp.int32, (n_new, 1), 0)
        kk = jnp.minimum(topk, qpos + 1).astype(F32)
        _topk_bias(key_ref, bias_ref.at[0], tri_ref, n_chunks + 1, kk, axis=1)


def _dec_select(page_table, qi, wi, cache_ki, layer, kinew):
    nseq, n_pages = page_table.shape
    n_new = qi.shape[1] // IDX_HEADS
    n_chunks = n_pages // ODD_PAGES
    topk = min(DSA_TOPK, (n_pages * PAGE_SIZE + n_new) // 4)

    def page_spec(r):
        return pl.BlockSpec((None, None, PAGE_SIZE, IDX_DIM),
                            lambda s, p, pt: (pt[s, jnp.minimum(p, n_chunks - 1) * ODD_PAGES + r], layer, 0, 0))

    return pl.pallas_call(
        functools.partial(_dec_select_kernel, n_new=n_new, n_chunks=n_chunks, topk=topk),
        grid_spec=pltpu.PrefetchScalarGridSpec(
            num_scalar_prefetch=1,
            grid=(nseq, n_chunks + 1),
            in_specs=[pl.BlockSpec((1, IDX_HEADS * n_new, IDX_DIM), lambda s, p, pt: (s, 0, 0)),
                      pl.BlockSpec((1, IDX_HEADS * n_new, 1), lambda s, p, pt: (s, 0, 0))]
            + [page_spec(r) for r in range(ODD_PAGES)]
            + [pl.BlockSpec((1, PAGE_SIZE, IDX_DIM), lambda s, p, pt: (s, 0, 0))],
            out_specs=pl.BlockSpec((1, n_chunks + 1, n_new, ODD_CHUNK), lambda s, p, pt: (s, 0, 0, 0)),
            scratch_shapes=[pltpu.VMEM((n_chunks + 1, n_new, ODD_CHUNK), jnp.int32),
                            pltpu.VMEM((ODD_CHUNK, ODD_CHUNK), BF16)],
        ),
        out_shape=jax.ShapeDtypeStruct((nseq, n_chunks + 1, n_new, ODD_CHUNK), F32),
        compiler_params=_params(("parallel", "arbitrary")),
        name="dec_select",
    )(page_table, qi, wi, *([cache_ki] * ODD_PAGES), kinew)


def _dec_odd_kernel(pt_ref, qbd_ref, bias_ref, *refs, n_new, n_chunks):
    kpages = refs[:ODD_PAGES]
    vpages = refs[ODD_PAGES:2 * ODD_PAGES]
    knew_ref, vnew_ref, o_ref, m_ref, l_ref, acc_ref = refs[2 * ODD_PAGES:]
    p = pl.program_id(1)
    qbd = qbd_ref[0]

    @pl.when(p == 0)
    def _():
        m_ref[...] = jnp.full_like(m_ref, NEG)
        l_ref[...] = jnp.zeros_like(l_ref)
        acc_ref[...] = jnp.zeros_like(acc_ref)

    def attend(k16, v16, bias):
        s = _dot_nt(qbd, k16) + jnp.tile(bias, (C_HEADS, 1))
        m_prev = m_ref[...]
        m_new = jnp.maximum(m_prev, jnp.max(s, axis=1, keepdims=True))
        alpha = jnp.exp(m_prev - m_new)
        pr = jnp.exp(s - m_new)
        l_ref[...] = alpha * l_ref[...] + jnp.sum(pr, axis=1, keepdims=True)
        acc_ref[...] = alpha * acc_ref[...] + _dot(pr.astype(BF16), v16)
        m_ref[...] = m_new

    @pl.when(p < n_chunks)
    def _():
        k16 = jnp.concatenate([r[...] for r in kpages], axis=0).astype(BF16)
        v16 = jnp.concatenate([r[...] for r in vpages], axis=0).astype(BF16)
        attend(k16, v16, bias_ref[0, 0])

    @pl.when(p == n_chunks)
    def _():
        attend(knew_ref[0], vnew_ref[0], bias_ref[0, 0, :, 0:NEW_PAD])
        r = acc_ref[...] / l_ref[...]
        group = C_HEADS // C_KV_HEADS
        for h in range(C_HEADS):
            g = h // group
            o_ref[0, :, h * HEAD_DIM:(h + 1) * HEAD_DIM] = (
                r[h * n_new:(h + 1) * n_new, g * HEAD_DIM:(g + 1) * HEAD_DIM].astype(BF16))


def _dec_odd(page_table, qbd, bias, cache_k, cache_v, layer, knew, vnew):
    nseq, n_pages = page_table.shape
    n_new = qbd.shape[1] // C_HEADS
    n_chunks = n_pages // ODD_PAGES

    def page_spec(r):
        return pl.BlockSpec((None, None, PAGE_SIZE, C_KV),
                            lambda s, p, pt: (pt[s, jnp.minimum(p, n_chunks - 1) * ODD_PAGES + r], layer, 0, 0))

    seq_spec = lambda shape: pl.BlockSpec((1,) + shape, lambda s, p, pt: (s, 0, 0))
    rows = C_HEADS * n_new
    return pl.pallas_call(
        functools.partial(_dec_odd_kernel, n_new=n_new, n_chunks=n_chunks),
        grid_spec=pltpu.PrefetchScalarGridSpec(
            num_scalar_prefetch=1,
            grid=(nseq, n_chunks + 1),
            in_specs=[seq_spec((rows, C_KV)),
                      pl.BlockSpec((1, 1, n_new, ODD_CHUNK), lambda s, p, pt: (s, p, 0, 0))]
            + [page_spec(r) for r in range(ODD_PAGES)] + [page_spec(r) for r in range(ODD_PAGES)]
            + [seq_spec((NEW_PAD, C_KV)), seq_spec((NEW_PAD, C_KV))],
            out_specs=seq_spec((n_new, C_Q)),
            scratch_shapes=[pltpu.VMEM((rows, 1), F32), pltpu.VMEM((rows, 1), F32), pltpu.VMEM((rows, C_KV), F32)],
        ),
        out_shape=jax.ShapeDtypeStruct((nseq, n_new, C_Q), BF16),
        compiler_params=_params(("parallel", "arbitrary")),
        name="dec_odd",
    )(page_table, qbd, bias, *([cache_k] * ODD_PAGES), *([cache_v] * ODD_PAGES), knew, vnew)


ROW_TILE = 1024
FFN_TF = 256


def _pad_rows(a, rows):
    return jnp.pad(a, ((0, 0), (0, rows - a.shape[1]), (0, 0)))


def _gqa_rows(q):
    s, t, nh, d = q.shape
    group = nh // C_KV_HEADS
    keep = (jnp.arange(nh)[:, None] // group) == jnp.arange(C_KV_HEADS)[None, :]
    qh = jnp.swapaxes(q, 1, 2)
    out = jnp.where(keep[None, :, None, :, None], qh[:, :, :, None, :], 0)
    return out.reshape(s, nh * t, C_KV_HEADS * d).astype(q.dtype)


def kernel(x_prompt, x_sample, cache_k_even, cache_v_even, cache_k_odd, cache_v_odd, cache_kidx_odd, state_conv,
           page_table, g_mix, g_ffn, g_final, w_in_even, w_out_even, lam_even, subln_even, w_in_odd, w_out_odd,
           w_gate, w_up, conv_w, conv_b, w_down):
    bsz, seq, d = x_prompt.shape
    nseq, n_new, _ = x_sample.shape
    depth = g_mix.shape[0]
    ff = w_gate.shape[-1]
    past_len = page_table.shape[1] * PAGE_SIZE
    bf = lambda a: a.astype(BF16)

    cos_p, sin_p = _rope_tables(jnp.arange(seq, dtype=jnp.int32))
    cos_s, sin_s = _rope_tables(jnp.tile(past_len + jnp.arange(n_new, dtype=jnp.int32), nseq))
    ecols, ocols = _even_cols(), _odd_cols()
    qb0 = 2 * A_QK + A_V
    tm = min(ROW_TILE, seq)
    ms = nseq * n_new

    hp = x_prompt.reshape(bsz * seq, d)
    hs = x_sample.reshape(ms, d)
    ke_p, ve_p, ko_p, vo_p, kio_p, cs_p = [], [], [], [], [], []
    ke_s, ve_s, ko_s, vo_s, kio_s, cs_s = [], [], [], [], [], []
    for layer in range(depth):
        e = layer // 2
        g = g_mix[layer][None]
        if layer % 2 == 0:
            w16, wo = bf(w_in_even[e]), bf(w_out_even[e])
            lam_init = 0.8 - 0.6 * math.exp(-0.3 * layer)
            lp = lam_even[e].astype(F32)
            lam = (jnp.exp(jnp.sum(lp[0] * lp[1])) - jnp.exp(jnp.sum(lp[2] * lp[3])) + lam_init).reshape(1)
            sg = subln_even[e][None]
            out_scale = 1.0 - lam_init

            p32, p16 = _inproj(hp, g, w16, cos_p, sin_p, *ecols, tm=tm)
            p32, p16 = p32.reshape(bsz, seq, -1), p16.reshape(bsz, seq, -1)
            pair = jnp.arange(2 * A_HEADS) % 2
            qat = _heads_t(p16[..., :A_QK].reshape(bsz, seq, 2 * A_HEADS, HEAD_DIM), pair)
            vat = jnp.swapaxes(p16[..., 2 * A_QK:qb0], 1, 2)
            oat = _diff_attn(lam, qat, p16, vat, sg.reshape(A_VDIM, 1), out_scale, tq=256, tk=512)
            kmean = _blockmean(p32, (qb0 + B_W) // B_W, B_W, nb_step=min(8, seq // MOBA_BLOCK))
            qbt = _heads_t(p16[..., qb0:qb0 + B_W].reshape(bsz, seq, B_HEADS, HEAD_DIM), pair)
            q32t = jnp.swapaxes(p32[..., qb0:qb0 + B_W], 1, 2)
            vbt = jnp.swapaxes(p16[..., qb0 + 2 * B_W:], 1, 2)
            obt = _moba_attn(qbt, q32t, p16, vbt, kmean, sub=2)
            o = jnp.swapaxes(jnp.concatenate([oat, obt], axis=1), 1, 2)
            hp = _outproj([o.reshape(bsz * seq, -1)], [wo], hp, tm=tm)
            ke_p.append(jnp.concatenate([p32[..., A_QK:2 * A_QK], p32[..., qb0 + B_W:qb0 + 2 * B_W]], -1))
            ve_p.append(jnp.concatenate([p32[..., 2 * A_QK:qb0], p32[..., qb0 + 2 * B_W:]], -1))

            s32, s16 = _inproj(hs, g, w16, cos_s, sin_s, *ecols, tm=ms)
            s32, s16 = s32.reshape(nseq, n_new, -1), s16.reshape(nseq, n_new, -1)
            qbd = _block_diag_rows(jnp.concatenate([s16[..., :A_QK], s16[..., qb0:qb0 + B_W]], -1), 16, HEAD_DIM)
            knew = jnp.concatenate([s16[..., A_QK:2 * A_QK], s16[..., qb0 + B_W:qb0 + 2 * B_W]], -1)
            vnew = jnp.concatenate([s16[..., 2 * A_QK:qb0], s16[..., qb0 + 2 * B_W:]], -1)
            o = _dec_even(page_table, lam, qbd, cache_k_even, cache_v_even, e,
                          _pad_rows(knew, NEW_PAD), _pad_rows(vnew, NEW_PAD), sg, out_scale)
            hs = _outproj([o.reshape(ms, -1)], [wo], hs, tm=ms)
            ke_s.append(jnp.concatenate([s32[..., A_QK:2 * A_QK], s32[..., qb0 + B_W:qb0 + 2 * B_W]], -1))
            ve_s.append(jnp.concatenate([s32[..., 2 * A_QK:qb0], s32[..., qb0 + 2 * B_W:]], -1))
        else:
            w16 = bf(jnp.pad(w_in_odd[e], ((0, 0), (0, ODD_NPAD - ODD_N))))
            wo = bf(w_out_odd[e])
            kc, vc, kic, wic = C_Q, C_Q + C_KV, ODD_KI_COL, ODD_KI_COL + IDX_DIM

            p32, p16 = _inproj(hp, g, w16, cos_p, sin_p, *ocols, tm=tm)
            p32, p16 = p32.reshape(bsz, seq, -1), p16.reshape(bsz, seq, -1)
            tc = min(512, seq)
            qt = _heads_t(p16[..., :C_Q].reshape(bsz, seq, C_HEADS, HEAD_DIM),
                          (jnp.arange(C_HEADS) // (C_HEADS // C_KV_HEADS)) % 2)
            qit = _heads_t(p16[..., vc + C_KV:kic].reshape(bsz, seq, IDX_HEADS, IDX_DIM), jnp.zeros(IDX_HEADS, jnp.int32))
            wit = jnp.swapaxes(p32[..., wic:wic + IDX_HEADS], 1, 2)
            vt4 = jnp.swapaxes(p16[..., vc:vc + C_KV].reshape(bsz, seq // tc, tc, C_KV), 2, 3)
            ot = _dsa_attn(qt, qit, wit, p16, vt4, tq=256, tc=tc)
            hp = _outproj([jnp.swapaxes(ot, 1, 2).reshape(bsz * seq, -1)], [wo], hp, tm=tm)
            ko_p.append(p32[..., kc:kc + C_KV])
            vo_p.append(p32[..., vc:vc + C_KV])
            kio_p.append(p32[..., kic:kic + IDX_DIM])

            s32, s16 = _inproj(hs, g, w16, cos_s, sin_s, *ocols, tm=ms)
            s32, s16 = s32.reshape(nseq, n_new, -1), s16.reshape(nseq, n_new, -1)
            qi = s16[..., vc + C_KV:kic].reshape(nseq, n_new, IDX_HEADS, IDX_DIM)
            qi = jnp.swapaxes(qi, 1, 2).reshape(nseq, IDX_HEADS * n_new, IDX_DIM)
            wi = jnp.swapaxes(s32[..., wic:wic + IDX_HEADS], 1, 2).reshape(nseq, IDX_HEADS * n_new, 1)
            bias = _dec_select(page_table, qi, wi, cache_kidx_odd, e, _pad_rows(s16[..., kic:kic + IDX_DIM], PAGE_SIZE))
            qbd = _gqa_rows(s16[..., :C_Q].reshape(nseq, n_new, C_HEADS, HEAD_DIM))
            o = _dec_odd(page_table, qbd, bias, cache_k_odd, cache_v_odd, e,
                         _pad_rows(s16[..., kc:kc + C_KV], NEW_PAD), _pad_rows(s16[..., vc:vc + C_KV], NEW_PAD))
            hs = _outproj([o.reshape(ms, -1)], [wo], hs, tm=ms)
            ko_s.append(s32[..., kc:kc + C_KV])
            vo_s.append(s32[..., vc:vc + C_KV])
            kio_s.append(s32[..., kic:kic + IDX_DIM])

        last = layer == depth - 1
        ffn_w = (g_ffn[layer][None], bf(w_gate[layer]), bf(w_up[layer]), conv_w[layer], conv_b[layer][None],
                 bf(w_down[layer]), g_final[None])
        hp, tail = _ffn(hp, *ffn_w, seq_len=seq, tm=tm, tf=FFN_TF, final_norm=last)
        cs_p.append(tail.reshape(bsz, seq // tm, 8, ff)[:, -1, 8 - (CONV_W - 1):])
        hs, gout = _ffn_s(hs, state_conv[layer], *ffn_w, seq_len=n_new, tf=FFN_TF, final_norm=last)
        cs_s.append(gout.reshape(nseq, n_new, ff)[:, n_new - (CONV_W - 1):])

    def to_pages(rows):
        r = jnp.stack(rows, 1)
        b, nl, s, w = r.shape
        return r.reshape(b, nl, s // PAGE_SIZE, PAGE_SIZE, w).transpose(0, 2, 1, 3, 4)

    return (hp.reshape(bsz, seq, d), hs.reshape(nseq, n_new, d),
            to_pages(ke_p), to_pages(ve_p), to_pages(ko_p), to_pages(vo_p), to_pages(kio_p), jnp.stack(cs_p, 0),
            jnp.stack(ke_s, 1), jnp.stack(ve_s, 1), jnp.stack(ko_s, 1), jnp.stack(vo_s, 1), jnp.stack(kio_s, 1),
            jnp.stack(cs_s, 0))
```

```python
import functools
import math

import jax
import jax.numpy as jnp
from jax import lax
from jax.experimental import pallas as pl
from jax.experimental.pallas import tpu as pltpu

F32 = jnp.float32
BF16 = jnp.bfloat16

HEAD_DIM = 64
A_HEADS = 4
A_VDIM = 2 * HEAD_DIM
B_HEADS = 8
MOBA_BLOCK = 256
MOBA_TOPK = 3
C_HEADS = 16
C_KV_HEADS = 4
IDX_HEADS = 8
IDX_DIM = 64
DSA_TOPK = 256
CONV_W = 3
ROPE_THETA = 10000.0
EPS = 1e-6
PAGE_SIZE = 128
A_QK = A_HEADS * 2 * HEAD_DIM
A_V = A_HEADS * A_VDIM
B_W = B_HEADS * HEAD_DIM
C_Q = C_HEADS * HEAD_DIM
C_KV = C_KV_HEADS * HEAD_DIM
IDX_Q = IDX_HEADS * IDX_DIM
IDX_W_SCALE = IDX_Q ** -0.5
ATTN_SCALE = HEAD_DIM ** -0.5

LANES = 128
NEG = -1e30
VMEM_LIMIT = 56 * 1024 * 1024

EVEN_N = 2 * A_QK + A_V + 3 * B_W
ODD_N = C_Q + 2 * C_KV + IDX_Q + IDX_DIM + IDX_HEADS
ODD_NPAD = 2304
ODD_KI_COL = C_Q + 2 * C_KV + IDX_Q
PROJ_TN = 768


def _params(sem, vmem=VMEM_LIMIT):
    return pltpu.CompilerParams(dimension_semantics=sem, vmem_limit_bytes=vmem)


def _dot_nt(a, b):
    return lax.dot_general(a, b, (((1,), (1,)), ((), ())), preferred_element_type=F32)


def _dot(a, b):
    return jnp.dot(a, b, preferred_element_type=F32)


def _rms(x, g):
    return x * lax.rsqrt(jnp.mean(x * x, axis=-1, keepdims=True) + EPS) * g


def _inproj_kernel(x_ref, g_ref, w_ref, cos_ref, sin_ref, rmask_ref, s32_ref, s16_ref,
                   o32_ref, o16_ref, xn_ref):
    @pl.when(pl.program_id(1) == 0)
    def _():
        xn_ref[...] = _rms(x_ref[...], g_ref[...]).astype(BF16)

    y = _dot(xn_ref[...], w_ref[...])
    tn = y.shape[1]
    reps = tn // LANES
    cos = jnp.tile(cos_ref[...], (1, reps))
    sin = jnp.tile(sin_ref[...], (1, reps))
    lane = lax.broadcasted_iota(jnp.int32, y.shape, 1)
    first_half = (lane % HEAD_DIM) < (HEAD_DIM // 2)
    partner = jnp.where(first_half, pltpu.roll(y, tn - HEAD_DIM // 2, 1), pltpu.roll(y, HEAD_DIM // 2, 1))
    roped = y * cos + partner * sin
    out = jnp.where(rmask_ref[...] > 0.5, roped, y) * s32_ref[...]
    o32_ref[...] = out
    o16_ref[...] = (out * s16_ref[...]).astype(BF16)


def _inproj(x, g, w16, cos, sin, rmask, s32, s16, tm):
    m, d = x.shape
    n = w16.shape[1]
    tn = PROJ_TN
    tblocks = cos.shape[0] // tm
    return pl.pallas_call(
        _inproj_kernel,
        grid=(m // tm, n // tn),
        in_specs=[
            pl.BlockSpec((tm, d), lambda i, j: (i, 0)),
            pl.BlockSpec((1, d), lambda i, j: (0, 0)),
            pl.BlockSpec((d, tn), lambda i, j: (0, j)),
            pl.BlockSpec((tm, LANES), lambda i, j: (i % tblocks, 0)),
            pl.BlockSpec((tm, LANES), lambda i, j: (i % tblocks, 0)),
            pl.BlockSpec((1, tn), lambda i, j: (0, j)),
            pl.BlockSpec((1, tn), lambda i, j: (0, j)),
            pl.BlockSpec((1, tn), lambda i, j: (0, j)),
        ],
        out_specs=[
            pl.BlockSpec((tm, tn), lambda i, j: (i, j)),
            pl.BlockSpec((tm, tn), lambda i, j: (i, j)),
        ],
        out_shape=[jax.ShapeDtypeStruct((m, n), F32), jax.ShapeDtypeStruct((m, n), BF16)],
        scratch_shapes=[pltpu.VMEM((tm, d), BF16)],
        compiler_params=_params(("parallel", "arbitrary")),
        name="inproj",
    )(x, g, w16, cos, sin, rmask, s32, s16)


def _rope_tables(pos):
    half = HEAD_DIM // 2
    inv = ROPE_THETA ** (-jnp.arange(half, dtype=F32) * 2.0 / HEAD_DIM)
    ang = pos.astype(F32)[:, None] * inv[None, :]
    cos = jnp.tile(jnp.cos(ang), (1, LANES // half))
    sin = jnp.tile(jnp.concatenate([-jnp.sin(ang), jnp.sin(ang)], axis=1), (1, LANES // HEAD_DIM))
    return cos, sin


def _col_rows(n, rope_ranges, scale32, scale16):
    cols = jnp.arange(n)
    rmask = jnp.zeros((n,), F32)
    for lo, hi in rope_ranges:
        rmask = jnp.where((cols >= lo) & (cols < hi), 1.0, rmask)
    s32 = jnp.ones((n,), F32)
    for lo, hi, v in scale32:
        s32 = jnp.where((cols >= lo) & (cols < hi), v, s32)
    s16 = jnp.ones((n,), F32)
    for lo, hi, v in scale16:
        s16 = jnp.where((cols >= lo) & (cols < hi), v, s16)
    return rmask[None], s32[None], s16[None]


def _even_cols():
    o = A_QK + A_QK + A_V
    return _col_rows(EVEN_N, [(0, 2 * A_QK), (o, o + 2 * B_W)], [],
                     [(0, A_QK, ATTN_SCALE), (o, o + B_W, ATTN_SCALE)])


def _odd_cols():
    return _col_rows(ODD_NPAD, [(0, C_Q + C_KV), (C_Q + 2 * C_KV, ODD_KI_COL + IDX_DIM)],
                     [(ODD_KI_COL + IDX_DIM, ODD_N, IDX_W_SCALE)], [(0, C_Q, ATTN_SCALE)])


def _outproj_kernel(*refs, n_in):
    xs, ws, h_ref, o_ref = refs[:n_in], refs[n_in:2 * n_in], refs[2 * n_in], refs[2 * n_in + 1]
    acc = h_ref[...]
    for x_ref, w_ref in zip(xs, ws):
        acc = acc + _dot(x_ref[...], w_ref[...])
    o_ref[...] = acc


def _outproj(xs, ws, h, tm):
    m, d = h.shape
    n_in = len(xs)
    in_specs = [pl.BlockSpec((tm, x.shape[1]), lambda i: (i, 0)) for x in xs]
    in_specs += [pl.BlockSpec(w.shape, lambda i: (0, 0)) for w in ws]
    in_specs += [pl.BlockSpec((tm, d), lambda i: (i, 0))]
    return pl.pallas_call(
        functools.partial(_outproj_kernel, n_in=n_in),
        grid=(m // tm,),
        in_specs=in_specs,
        out_specs=pl.BlockSpec((tm, d), lambda i: (i, 0)),
        out_shape=jax.ShapeDtypeStruct((m, d), F32),
        compiler_params=_params(("parallel",)),
        name="outproj",
    )(*xs, *ws, h)


FFN_HALO = 16


def _ffn_kernel(h_ref, halo_ref, g_ref, wg_ref, wu_ref, cw_ref, cb_ref, wd_ref, gf_ref,
                o_ref, tail_ref, xn_ref, gext_ref, acc_ref, *, tiles_per_seq, final_norm):
    i, f = pl.program_id(0), pl.program_id(1)
    tm = h_ref.shape[0]

    @pl.when(f == 0)
    def _():
        xn_ref[0:FFN_HALO, :] = _rms(halo_ref[...], g_ref[...]).astype(BF16)
        xn_ref[FFN_HALO:, :] = _rms(h_ref[...], g_ref[...]).astype(BF16)
        acc_ref[...] = jnp.zeros_like(acc_ref)

    xn = xn_ref[...]
    gext = _dot(xn, wg_ref[...])
    first = (i % tiles_per_seq) == 0
    row = lax.broadcasted_iota(jnp.int32, gext.shape, 0)
    gext_ref[...] = jnp.where(jnp.logical_and(first, row < FFN_HALO), 0.0, gext)
    g = gext_ref[FFN_HALO:, :]
    p1 = gext_ref[pl.ds(FFN_HALO - 1, tm), :]
    p2 = gext_ref[pl.ds(FFN_HALO - 2, tm), :]
    cw = cw_ref[...]
    c = cb_ref[...] + p2 * cw[0:1] + p1 * cw[1:2] + g * cw[2:3]
    u = _dot(xn[FFN_HALO:], wu_ref[...])
    a = (c * jax.nn.sigmoid(c) * u).astype(BF16)
    acc_ref[...] += _dot(a, wd_ref[...])
    tail_ref[0] = g[tm - 8:, :]

    @pl.when(f == pl.num_programs(1) - 1)
    def _():
        out = h_ref[...] + acc_ref[...]
        if final_norm:
            out = _rms(out, gf_ref[...])
        o_ref[...] = out


def _ffn(h, g, wg, wu, cw, cb, wd, gf, seq_len, tm, tf, final_norm):
    m, d = h.shape
    ff = wg.shape[1]
    nt = m // tm
    hb = tm // FFN_HALO
    out, tail = pl.pallas_call(
        functools.partial(_ffn_kernel, tiles_per_seq=seq_len // tm, final_norm=final_norm),
        grid=(nt, ff // tf),
        in_specs=[
            pl.BlockSpec((tm, d), lambda i, f: (i, 0)),
            pl.BlockSpec((FFN_HALO, d), lambda i, f: (jnp.maximum(i * hb - 1, 0), 0)),
            pl.BlockSpec((1, d), lambda i, f: (0, 0)),
            pl.BlockSpec((d, tf), lambda i, f: (0, f)),
            pl.BlockSpec((d, tf), lambda i, f: (0, f)),
            pl.BlockSpec((CONV_W, tf), lambda i, f: (0, f)),
            pl.BlockSpec((1, tf), lambda i, f: (0, f)),
            pl.BlockSpec((tf, d), lambda i, f: (f, 0)),
            pl.BlockSpec((1, d), lambda i, f: (0, 0)),
        ],
        out_specs=[
            pl.BlockSpec((tm, d), lambda i, f: (i, 0)),
            pl.BlockSpec((1, 8, tf), lambda i, f: (i, 0, f)),
        ],
        out_shape=[jax.ShapeDtypeStruct((m, d), F32), jax.ShapeDtypeStruct((nt, 8, ff), F32)],
        scratch_shapes=[pltpu.VMEM((FFN_HALO + tm, d), BF16), pltpu.VMEM((FFN_HALO + tm, tf), F32),
                        pltpu.VMEM((tm, d), F32)],
        compiler_params=_params(("parallel", "arbitrary")),
        name="ffn",
    )(h, h, g, wg, wu, cw, cb, wd, gf)
    return out, tail


def _ffn_s_kernel(h_ref, b1_ref, b2_ref, g_ref, wg_ref, wu_ref, cw_ref, cb_ref, wd_ref, gf_ref,
                  o_ref, gout_ref, xn_ref, acc_ref, *, seq_len, final_norm):
    f = pl.program_id(0)

    @pl.when(f == 0)
    def _():
        xn_ref[...] = _rms(h_ref[...], g_ref[...]).astype(BF16)
        acc_ref[...] = jnp.zeros_like(acc_ref)

    xn = xn_ref[...]
    g = _dot(xn, wg_ref[...])
    t = lax.broadcasted_iota(jnp.int32, g.shape, 0) % seq_len
    p1 = jnp.where(t < 1, b1_ref[...], pltpu.roll(g, 1, 0))
    p2 = jnp.where(t < 2, b2_ref[...], pltpu.roll(g, 2, 0))
    cw = cw_ref[...]
    c = cb_ref[...] + p2 * cw[0:1] + p1 * cw[1:2] + g * cw[2:3]
    u = _dot(xn, wu_ref[...])
    a = (c * jax.nn.sigmoid(c) * u).astype(BF16)
    acc_ref[...] += _dot(a, wd_ref[...])
    gout_ref[...] = g

    @pl.when(f == pl.num_programs(0) - 1)
    def _():
        out = h_ref[...] + acc_ref[...]
        if final_norm:
            out = _rms(out, gf_ref[...])
        o_ref[...] = out


def _ffn_s(h, buf, g, wg, wu, cw, cb, wd, gf, seq_len, tf, final_norm):
    m, d = h.shape
    ff = wg.shape[1]
    nseq = m // seq_len
    zeros = jnp.zeros((nseq, seq_len - 2, ff), F32)
    b1 = jnp.concatenate([buf[:, 1:2], jnp.zeros((nseq, 1, ff), F32), zeros], axis=1).reshape(m, ff)
    b2 = jnp.concatenate([buf[:, 0:1], buf[:, 1:2], zeros], axis=1).reshape(m, ff)
    out, gout = pl.pallas_call(
        functools.partial(_ffn_s_kernel, seq_len=seq_len, final_norm=final_norm),
        grid=(ff // tf,),
        in_specs=[
            pl.BlockSpec((m, d), lambda f: (0, 0)),
            pl.BlockSpec((m, tf), lambda f: (0, f)),
            pl.BlockSpec((m, tf), lambda f: (0, f)),
            pl.BlockSpec((1, d), lambda f: (0, 0)),
            pl.BlockSpec((d, tf), lambda f: (0, f)),
            pl.BlockSpec((d, tf), lambda f: (0, f)),
            pl.BlockSpec((CONV_W, tf), lambda f: (0, f)),
            pl.BlockSpec((1, tf), lambda f: (0, f)),
            pl.BlockSpec((tf, d), lambda f: (f, 0)),
            pl.BlockSpec((1, d), lambda f: (0, 0)),
        ],
        out_specs=[
            pl.BlockSpec((m, d), lambda f: (0, 0)),
            pl.BlockSpec((m, tf), lambda f: (0, f)),
        ],
        out_shape=[jax.ShapeDtypeStruct((m, d), F32), jax.ShapeDtypeStruct((m, ff), F32)],
        scratch_shapes=[pltpu.VMEM((m, d), BF16), pltpu.VMEM((m, d), F32)],
        compiler_params=_params(("arbitrary",)),
        name="ffn_sample",
    )(h, b1, b2, g, wg, wu, cw, cb, wd, gf)
    return out, gout


SLAB = 2 * HEAD_DIM


def _heads_t(q, slot):
    b, l, nh, dd = q.shape
    keep = jnp.asarray(slot)[:, None] == jnp.arange(2)[None, :]
    qt = jnp.transpose(q, (0, 2, 3, 1))
    out = jnp.where(keep[None, :, :, None, None], qt[:, :, None, :, :], jnp.zeros((), q.dtype))
    return out.reshape(b, nh * SLAB, l)


def _flash_update_t(sts, vts, m_ref, l_ref, acc_ref, idx):
    m_prev = m_ref[idx]
    m_new = functools.reduce(jnp.maximum, [jnp.max(s, axis=0, keepdims=True) for s in sts], m_prev)
    alpha = jnp.exp(m_prev - m_new)
    ps = [jnp.exp(s - m_new) for s in sts]
    l_ref[idx] = alpha * l_ref[idx] + functools.reduce(jnp.add, [jnp.sum(p, axis=0, keepdims=True) for p in ps])
    acc = alpha * acc_ref[idx]
    for vt, p in zip(vts, ps):
        acc = acc + _dot(vt, p.astype(BF16))
    acc_ref[idx] = acc
    m_ref[idx] = m_new


def _flash_init(m_ref, l_ref, acc_ref):
    m_ref[...] = jnp.full_like(m_ref, NEG)
    l_ref[...] = jnp.zeros_like(l_ref)
    acc_ref[...] = jnp.zeros_like(acc_ref)


def _diff_kernel(lam_ref, qt_ref, k_ref, vt_ref, g_ref, o_ref, s_ref, m_ref, l_ref, acc_ref, *, tq, tk, out_scale):
    i, j = pl.program_id(1), pl.program_id(2)
    jlast = ((i + 1) * tq - 1) // tk

    @pl.when(j == 0)
    def _():
        _flash_init(m_ref, l_ref, acc_ref)

    def step(masked):
        if masked:
            kpos = j * tk + lax.broadcasted_iota(jnp.int32, (tk, 2 * tq), 0)
            qpos = i * tq + lax.broadcasted_iota(jnp.int32, (tk, 2 * tq), 1) % tq
            ok = kpos <= qpos
        def scores(h):
            q2 = jnp.concatenate([qt_ref[0, m * SLAB:(m + 1) * SLAB, :] for m in (2 * h, 2 * h + 1)], axis=1)
            s = _dot(k_ref[0, :, h * SLAB:(h + 1) * SLAB], q2)
            s_ref[h % 2] = jnp.where(ok, s, NEG) if masked else s

        scores(0)
        for h in range(A_HEADS):
            if h + 1 < A_HEADS:
                scores(h + 1)
            _flash_update_t([s_ref[h % 2]], [vt_ref[0, h * A_VDIM:(h + 1) * A_VDIM, :]], m_ref, l_ref, acc_ref, h)

    crosses = (j + 1) * tk - 1 > i * tq

    @pl.when(jnp.logical_and(j <= jlast, crosses))
    def _():
        step(True)

    @pl.when(jnp.logical_and(j <= jlast, jnp.logical_not(crosses)))
    def _():
        step(False)

    @pl.when(j == jlast)
    def _():
        lam = lam_ref[0]
        for h in range(A_HEADS):
            r = acc_ref[h] / l_ref[h]
            o = r[:, :tq] - lam * r[:, tq:]
            o = o * lax.rsqrt(jnp.mean(o * o, axis=0, keepdims=True) + EPS) * g_ref[...] * out_scale
            o_ref[0, h * A_VDIM:(h + 1) * A_VDIM, :] = o.astype(BF16)


def _diff_attn(lam, qt, qkv16, vt, subln_col, out_scale, tq, tk):
    b, l, _ = qkv16.shape
    nq, nk = l // tq, l // tk
    jmax = lambda i, j: jnp.minimum(j, ((i + 1) * tq - 1) // tk)
    return pl.pallas_call(
        functools.partial(_diff_kernel, tq=tq, tk=tk, out_scale=out_scale),
        grid_spec=pltpu.PrefetchScalarGridSpec(
            num_scalar_prefetch=1,
            grid=(b, nq, nk),
            in_specs=[
                pl.BlockSpec((1, 2 * A_HEADS * SLAB, tq), lambda bi, i, j, lam_ref: (bi, 0, i)),
                pl.BlockSpec((1, tk, A_QK), lambda bi, i, j, lam_ref: (bi, jmax(i, j), 1)),
                pl.BlockSpec((1, A_V, tk), lambda bi, i, j, lam_ref: (bi, 0, jmax(i, j))),
                pl.BlockSpec((A_VDIM, 1), lambda bi, i, j, lam_ref: (0, 0)),
            ],
            out_specs=pl.BlockSpec((1, A_V, tq), lambda bi, i, j, lam_ref: (bi, 0, i)),
            scratch_shapes=[pltpu.VMEM((2, tk, 2 * tq), F32),
                            pltpu.VMEM((A_HEADS, 1, 2 * tq), F32), pltpu.VMEM((A_HEADS, 1, 2 * tq), F32),
                            pltpu.VMEM((A_HEADS, A_VDIM, 2 * tq), F32)],
        ),
        out_shape=jax.ShapeDtypeStruct((b, A_V, l), BF16),
        compiler_params=_params(("parallel", "parallel", "arbitrary")),
        name="diff_attn",
    )(lam, qt, qkv16, vt, subln_col)


def _blockmean_kernel(k_ref, o_ref):
    nb = o_ref.shape[1]
    for n in range(nb):
        o_ref[0, n:n + 1, :] = jnp.mean(k_ref[0, n * MOBA_BLOCK:(n + 1) * MOBA_BLOCK, :], axis=0, keepdims=True)


def _blockmean(x32, col_block, width, nb_step):
    b, l, _ = x32.shape
    nb = l // MOBA_BLOCK
    return pl.pallas_call(
        _blockmean_kernel,
        grid=(b, nb // nb_step),
        in_specs=[pl.BlockSpec((1, nb_step * MOBA_BLOCK, width), lambda bi, i: (bi, i, col_block))],
        out_specs=pl.BlockSpec((1, nb_step, width), lambda bi, i: (bi, i, 0)),
        out_shape=jax.ShapeDtypeStruct((b, nb, width), F32),
        compiler_params=_params(("parallel", "parallel")),
        name="blockmean",
    )(x32)


def _top3_select(gate, n_valid, axis=1):
    nb = gate.shape[axis]
    idx = lax.broadcasted_iota(jnp.int32, gate.shape, axis)
    sel = jnp.zeros(gate.shape, F32)
    for r in range(MOBA_TOPK):
        mx = jnp.max(gate, axis=axis, keepdims=True)
        first = jnp.min(jnp.where(gate == mx, idx, nb), axis=axis, keepdims=True)
        pick = idx == first
        sel = jnp.where(jnp.logical_and(pick, r < n_valid), 1.0, sel)
        gate = jnp.where(pick, -jnp.inf, gate)
    return sel


def _dot_nt_f32(a, b):
    return lax.dot_general(a, b, (((1,), (1,)), ((), ())), preferred_element_type=F32,
                           precision=lax.Precision.HIGHEST)


def _dot_f32(a, b):
    return jnp.dot(a, b, preferred_element_type=F32, precision=lax.Precision.HIGHEST)


def _moba_kernel(qt_ref, q32t_ref, k_ref, vt_ref, km_ref, o_ref, sel_ref, s_ref, m_ref, l_ref, acc_ref, *, sub):
    i, j = pl.program_id(1), pl.program_id(2)
    tq = MOBA_BLOCK
    nb = km_ref.shape[1]
    jlast = i // sub

    @pl.when(j == 0)
    def _():
        _flash_init(m_ref, l_ref, acc_ref)
        blk = lax.broadcasted_iota(jnp.int32, (nb, tq), 0)
        for h in range(B_HEADS):
            rows = slice(h * HEAD_DIM, (h + 1) * HEAD_DIM)
            gate = _dot_f32(km_ref[0, :, rows], q32t_ref[0, rows, :])
            sel_ref[h] = _top3_select(jnp.where(blk < i, gate, -jnp.inf), i, axis=0)

    def sub_block(r, own):
        n = j * sub + r
        keys = slice(r * MOBA_BLOCK, (r + 1) * MOBA_BLOCK)
        if own:
            ok = (lax.broadcasted_iota(jnp.int32, (MOBA_BLOCK, 2 * tq), 0)
                  <= lax.broadcasted_iota(jnp.int32, (MOBA_BLOCK, 2 * tq), 1) % tq)
        def scores(p):
            q2 = jnp.concatenate([qt_ref[0, h * SLAB:(h + 1) * SLAB, :] for h in (2 * p, 2 * p + 1)], axis=1)
            s = _dot(k_ref[0, keys, p * SLAB:(p + 1) * SLAB], q2)
            if own:
                chosen = ok
            else:
                chosen = jnp.concatenate([sel_ref[h, pl.ds(n, 1), :] for h in (2 * p, 2 * p + 1)], axis=1) > 0.5
            s_ref[p % 2] = jnp.where(chosen, s, NEG)

        scores(0)
        for p in range(B_HEADS // 2):
            if p + 1 < B_HEADS // 2:
                scores(p + 1)
            _flash_update_t([s_ref[p % 2]], [vt_ref[0, p * SLAB:(p + 1) * SLAB, keys]], m_ref, l_ref, acc_ref, p)

    for r in range(sub):
        n = j * sub + r

        @pl.when(jnp.logical_and(j <= jlast, n < i))
        def _():
            sub_block(r, False)

        @pl.when(n == i)
        def _():
            sub_block(r, True)

    @pl.when(j == jlast)
    def _():
        for h in range(B_HEADS):
            rows = slice((h % 2) * HEAD_DIM, (h % 2 + 1) * HEAD_DIM)
            cols = slice((h % 2) * tq, (h % 2 + 1) * tq)
            o_ref[0, h * HEAD_DIM:(h + 1) * HEAD_DIM, :] = (acc_ref[h // 2, rows, cols] / l_ref[h // 2, :, cols]).astype(BF16)


def _moba_attn(qt, q32t, qkv16, vt, kmean, sub):
    b, l, _ = qkv16.shape
    nb = l // MOBA_BLOCK
    tk = sub * MOBA_BLOCK
    kcol = (2 * A_QK + A_V + B_W) // B_W
    jmax = lambda i, j: jnp.minimum(j, i // sub)
    return pl.pallas_call(
        functools.partial(_moba_kernel, sub=sub),
        grid=(b, nb, l // tk),
        in_specs=[
            pl.BlockSpec((1, B_HEADS * SLAB, MOBA_BLOCK), lambda bi, i, j: (bi, 0, i)),
            pl.BlockSpec((1, B_W, MOBA_BLOCK), lambda bi, i, j: (bi, 0, i)),
            pl.BlockSpec((1, tk, B_W), lambda bi, i, j: (bi, jmax(i, j), kcol)),
            pl.BlockSpec((1, B_W, tk), lambda bi, i, j: (bi, 0, jmax(i, j))),
            pl.BlockSpec((1, nb, B_W), lambda bi, i, j: (bi, 0, 0)),
        ],
        out_specs=pl.BlockSpec((1, B_W, MOBA_BLOCK), lambda bi, i, j: (bi, 0, i)),
        out_shape=jax.ShapeDtypeStruct((b, B_W, l), BF16),
        scratch_shapes=[pltpu.VMEM((B_HEADS, nb, MOBA_BLOCK), F32),
                        pltpu.VMEM((2, MOBA_BLOCK, 2 * MOBA_BLOCK), F32),
                        pltpu.VMEM((B_HEADS // 2, 1, 2 * MOBA_BLOCK), F32),
                        pltpu.VMEM((B_HEADS // 2, 1, 2 * MOBA_BLOCK), F32),
                        pltpu.VMEM((B_HEADS // 2, SLAB, 2 * MOBA_BLOCK), F32)],
        compiler_params=_params(("parallel", "parallel", "arbitrary")),
        name="moba_attn",
    )(qt, q32t, qkv16, vt, kmean)


INT_MIN = -2 ** 31


def _order_key(score):
    bits = pltpu.bitcast(jnp.where(score == 0.0, 0.0, score), jnp.int32)
    return bits ^ ((bits >> 31) & 0x7FFFFFFF)


SUBLANES = 8


def _fold(x, axis):
    if axis == 1:
        part = x[:, 0:LANES]
        for s in range(1, x.shape[1] // LANES):
            part = part + x[:, s * LANES:(s + 1) * LANES]
        return part
    return jnp.sum(x.reshape(x.shape[0] // SUBLANES, SUBLANES, x.shape[1]), axis=0)


def _count(key_ref, nch, preds, axis):
    shape = key_ref.shape[1:]
    part = (shape[0], LANES) if axis == 1 else (SUBLANES, shape[1])

    def body(c, accs):
        blk = key_ref[c]
        return tuple(a + _fold(jnp.where(p(blk), 1.0, 0.0), axis) for a, p in zip(accs, preds))

    accs = lax.fori_loop(0, nch, body, tuple(jnp.zeros(part, F32) for _ in preds))
    return tuple(jnp.sum(a, axis=axis, keepdims=True) for a in accs)


def _kth_largest_key(key_ref, nch, kk, axis):
    def cond(carry):
        it, _, _, n_ge = carry
        return jnp.logical_and(it < 32, jnp.max(n_ge - kk) > 0.5)

    def search(carry):
        it, t_u, bit, n_ge = carry
        cand_u = t_u | bit
        cand = cand_u ^ INT_MIN
        cnt, = _count(key_ref, nch, (lambda blk: blk >= cand,), axis)
        keep = cnt >= kk
        return it + 1, jnp.where(keep, cand_u, t_u), lax.shift_right_logical(bit, 1), jnp.where(keep, cnt, n_ge)

    n_all, = _count(key_ref, nch, (lambda blk: blk > INT_MIN,), axis)
    _, t_u, _, n_ge = lax.while_loop(
        cond, search, (jnp.int32(0), jnp.zeros(kk.shape, jnp.int32), jnp.full(kk.shape, INT_MIN, jnp.int32), n_all))
    return jnp.maximum(t_u ^ INT_MIN, INT_MIN + 1), n_ge


def _topk_bias(key_ref, out_ref, tri_ref, nch, kk, axis):
    tc = key_ref.shape[1 + axis]
    t, n_ge = _kth_largest_key(key_ref, nch, kk, axis)
    ties = jnp.max(n_ge - kk) > 0.5

    def put(c, take):
        bias = jnp.where(take, 0.0, NEG)
        out_ref[c] = pltpu.bitcast(bias, jnp.int32) if out_ref.dtype == jnp.int32 else bias

    @pl.when(jnp.logical_not(ties))
    def _():
        def body(c, _):
            put(c, key_ref[c] >= t)
            return 0
        lax.fori_loop(0, nch, body, 0)

    @pl.when(ties)
    def _():
        n_gt, = _count(key_ref, nch, (lambda blk: blk > t,), axis)
        need = kk - n_gt
        r = lax.broadcasted_iota(jnp.int32, (tc, tc), 0)
        cidx = lax.broadcasted_iota(jnp.int32, (tc, tc), 1)
        tri_ref[...] = jnp.where((r <= cidx) if axis == 1 else (cidx <= r), 1.0, 0.0).astype(BF16)

        def body(c, seen):
            blk = key_ref[c]
            eqm = blk == t
            eq16 = jnp.where(eqm, 1.0, 0.0).astype(BF16)
            rank = seen + (_dot(eq16, tri_ref[...]) if axis == 1 else _dot(tri_ref[...], eq16))
            put(c, jnp.logical_or(blk > t, jnp.logical_and(eqm, rank <= need)))
            return seen + jnp.sum(jnp.where(eqm, 1.0, 0.0), axis=axis, keepdims=True)
        lax.fori_loop(0, nch, body, jnp.zeros(kk.shape, F32))


DSA_SPLIT = 1


def _dsa_kernel(qt_ref, qit_ref, wit_ref, k_ref, vt_ref, ki_ref, o_ref,
                key_ref, tri_ref, s_ref, m_ref, l_ref, acc_ref, *, tq, tc, topk):
    i = pl.program_id(1)
    nch = ((i + 1) * tq - 1) // tc + 1
    qpos = i * tq + lax.broadcasted_iota(jnp.int32, (1, tq), 1)
    kk = jnp.minimum(topk, qpos + 1).astype(F32)

    def score_chunk(c, _):
        r0 = pl.multiple_of(c * tc, tc)
        kic = ki_ref[0, pl.ds(r0, tc), :]
        score = jnp.zeros((tc, tq), F32)
        for h in range(IDX_HEADS):
            lg = _dot(kic, qit_ref[0, h * SLAB:(h + 1) * SLAB, :])
            score = score + wit_ref[0, h:h + 1, :] * jnp.maximum(lg, 0.0)
        kpos = r0 + lax.broadcasted_iota(jnp.int32, (tc, tq), 0)
        key_ref[c] = jnp.where(kpos <= qpos, _order_key(score), INT_MIN)
        return 0

    lax.fori_loop(0, nch, score_chunk, 0)
    _topk_bias(key_ref, key_ref, tri_ref, nch, kk, axis=0)
    _flash_init(m_ref, l_ref, acc_ref)
    group = C_HEADS // C_KV_HEADS

    def scores(c, g):
        r0 = pl.multiple_of(c * tc, tc)
        qg = jnp.concatenate([qt_ref[0, h * SLAB:(h + 1) * SLAB, :] for h in range(g * group, (g + 1) * group)], axis=1)
        bias = jnp.concatenate([pltpu.bitcast(key_ref[c], F32)] * group, axis=1)
        s_ref[g % 2] = _dot(k_ref[0, pl.ds(r0, tc), (g // 2) * SLAB:(g // 2 + 1) * SLAB], qg) + bias

    def attend(c, _):
        for g in range(C_KV_HEADS):
            if g + 1 < C_KV_HEADS:
                scores(c, g + 1)
            else:
                scores(jnp.minimum(c + 1, nch - 1), 0)
            _flash_update_t([s_ref[g % 2]], [vt_ref[0, c, g * HEAD_DIM:(g + 1) * HEAD_DIM, :]], m_ref, l_ref, acc_ref, g)
        return 0

    scores(0, 0)
    lax.fori_loop(0, nch, attend, 0)
    for h in range(C_HEADS):
        g, hh = h // group, h % group
        cols = slice(hh * tq, (hh + 1) * tq)
        o_ref[0, h * HEAD_DIM:(h + 1) * HEAD_DIM, :] = (acc_ref[g, :, cols] / l_ref[g, :, cols]).astype(BF16)


def _dsa_attn(qt, qit, wit, qkv16, vt4, tq, tc):
    b, l, _ = qkv16.shape
    topk = min(DSA_TOPK, l // 4)
    nc = l // tc
    gq = (C_HEADS // C_KV_HEADS) * tq
    return pl.pallas_call(
        functools.partial(_dsa_kernel, tq=tq, tc=tc, topk=topk),
        grid=(b, l // tq),
        in_specs=[
            pl.BlockSpec((1, C_HEADS * SLAB, tq), lambda bi, i: (bi, 0, i)),
            pl.BlockSpec((1, IDX_HEADS * SLAB, tq), lambda bi, i: (bi, 0, i)),
            pl.BlockSpec((1, IDX_HEADS, tq), lambda bi, i: (bi, 0, i)),
            pl.BlockSpec((1, l, C_KV), lambda bi, i: (bi, 0, C_Q // C_KV)),
            pl.BlockSpec((1, nc, C_KV, tc), lambda bi, i: (bi, 0, 0, 0)),
            pl.BlockSpec((1, l, SLAB), lambda bi, i: (bi, 0, ODD_KI_COL // SLAB)),
        ],
        out_specs=pl.BlockSpec((1, C_Q, tq), lambda bi, i: (bi, 0, i)),
        out_shape=jax.ShapeDtypeStruct((b, C_Q, l), BF16),
        scratch_shapes=[pltpu.VMEM((nc, tc, tq), jnp.int32), pltpu.VMEM((tc, tc), BF16),
                        pltpu.VMEM((2, tc, gq), F32),
                        pltpu.VMEM((C_KV_HEADS, 1, gq), F32), pltpu.VMEM((C_KV_HEADS, 1, gq), F32),
                        pltpu.VMEM((C_KV_HEADS, HEAD_DIM, gq), F32)],
        compiler_params=_params(("parallel", "arbitrary")),
        name="dsa_attn",
    )(qt, qit, wit, qkv16, vt4, qkv16)


NEW_PAD = 16
EVEN_PAGES = MOBA_BLOCK // PAGE_SIZE


def _fold_heads(x, n_groups, n_new):
    return jnp.sum(x.reshape(n_groups, n_new, x.shape[1]), axis=0)


def _dec_even_kernel(pt_ref, lam_ref, qbd_ref, *refs, n_new, n_blocks, out_scale):
    kpages = refs[:EVEN_PAGES]
    vpages = refs[EVEN_PAGES:2 * EVEN_PAGES]
    knew_ref, vnew_ref, g_ref, o_ref, ma_ref, la_ref, acca_ref, mb_ref, lb_ref, accb_ref, km_ref = refs[2 * EVEN_PAGES:]
    p = pl.program_id(1)
    ra = 8 * n_new
    qbd = qbd_ref[0]

    @pl.when(p == 0)
    def _():
        ma_ref[...] = jnp.full_like(ma_ref, NEG)
        la_ref[...] = jnp.zeros_like(la_ref)
        acca_ref[...] = jnp.zeros_like(acca_ref)
        km_ref[...] = jnp.zeros_like(km_ref)

    def attend(k16, v16, mask, blk):
        s = _dot_nt(qbd, k16)
        if mask is not None:
            s = jnp.where(mask, s, NEG)
        sa, sb = s[:ra], s[ra:]
        m_prev = ma_ref[...]
        m_new = jnp.maximum(m_prev, jnp.max(sa, axis=1, keepdims=True))
        alpha = jnp.exp(m_prev - m_new)
        pa = jnp.exp(sa - m_new)
        la_ref[...] = alpha * la_ref[...] + jnp.sum(pa, axis=1, keepdims=True)
        acca_ref[...] = alpha * acca_ref[...] + _dot(pa.astype(BF16), v16[:, :A_V])
        ma_ref[...] = m_new
        mb = jnp.max(sb, axis=1, keepdims=True)
        pb = jnp.exp(sb - mb)
        mb_ref[blk] = mb
        lb_ref[blk] = jnp.sum(pb, axis=1, keepdims=True)
        accb_ref[blk] = _dot(pb.astype(BF16), v16[:, A_V:])

    @pl.when(p < n_blocks)
    def _():
        k32 = jnp.concatenate([r[...] for r in kpages], axis=0)
        v32 = jnp.concatenate([r[...] for r in vpages], axis=0)
        km_ref[pl.ds(p, 1), :] = jnp.mean(k32[:, A_QK:], axis=0, keepdims=True)
        attend(k32.astype(BF16), v32.astype(BF16), None, p)

    @pl.when(p == n_blocks)
    def _():
        rows = 16 * n_new
        t = lax.broadcasted_iota(jnp.int32, (rows, NEW_PAD), 0) % n_new
        j = lax.broadcasted_iota(jnp.int32, (rows, NEW_PAD), 1)
        attend(knew_ref[0], vnew_ref[0], j <= t, n_blocks)

        lam = lam_ref[0]
        r = acca_ref[...] / la_ref[...]
        m_idx = lax.broadcasted_iota(jnp.int32, r.shape, 0) // n_new
        col = lax.broadcasted_iota(jnp.int32, r.shape, 1)
        coef = jnp.where(m_idx % 2 == 0, 1.0, -lam)
        oa = _fold_heads(jnp.where(col // A_VDIM == m_idx // 2, r * coef, 0.0), 8, n_new)
        for h in range(A_HEADS):
            cs = slice(h * A_VDIM, (h + 1) * A_VDIM)
            o_ref[0, :, cs] = (_rms(oa[:, cs], g_ref[...]) * out_scale).astype(BF16)

        nbp = km_ref.shape[0]
        gate = _dot_nt_f32(qbd[ra:, A_QK:].astype(F32), km_ref[...])
        blk = lax.broadcasted_iota(jnp.int32, gate.shape, 1)
        sel = _top3_select(jnp.where(blk < n_blocks, gate, -jnp.inf), n_blocks)
        m_run, l_run, acc = mb_ref[n_blocks], lb_ref[n_blocks], accb_ref[n_blocks]
        for n in range(n_blocks):
            chosen = sel[:, n:n + 1] > 0.5
            m_n = mb_ref[n]
            m_new = jnp.where(chosen, jnp.maximum(m_run, m_n), m_run)
            a_old = jnp.exp(m_run - m_new)
            a_n = jnp.where(chosen, jnp.exp(m_n - m_new), 0.0)
            l_run = a_old * l_run + a_n * lb_ref[n]
            acc = a_old * acc + a_n * accb_ref[n]
            m_run = m_new
        r = acc / l_run
        h_idx = lax.broadcasted_iota(jnp.int32, r.shape, 0) // n_new
        col = lax.broadcasted_iota(jnp.int32, r.shape, 1)
        ob = _fold_heads(jnp.where(col // HEAD_DIM == h_idx, r, 0.0), 8, n_new)
        o_ref[0, :, A_V:] = ob.astype(BF16)


def _block_diag_rows(q, n_maps, width):
    s, t, c = q.shape
    keep = (jnp.arange(c)[None, :] // width) == jnp.arange(n_maps)[:, None]
    return jnp.where(keep[None, :, None, :], q[:, None, :, :], 0).reshape(s, n_maps * t, c).astype(q.dtype)


def _dec_even(page_table, lam, qbd, cache_k, cache_v, layer, knew, vnew, subln_g, out_scale):
    nseq, n_pages = page_table.shape
    n_new = qbd.shape[1] // 16
    n_blocks = n_pages // EVEN_PAGES
    width = cache_k.shape[-1]

    def page_spec(r):
        return pl.BlockSpec((None, None, PAGE_SIZE, width),
                            lambda s, p, pt, lm: (pt[s, jnp.minimum(p, n_blocks - 1) * EVEN_PAGES + r], layer, 0, 0))

    seq_spec = lambda shape: pl.BlockSpec((1,) + shape, lambda s, p, pt, lm: (s, 0, 0))
    ra = 8 * n_new
    nbp = -(-(n_blocks + 1) // 8) * 8
    return pl.pallas_call(
        functools.partial(_dec_even_kernel, n_new=n_new, n_blocks=n_blocks, out_scale=out_scale),
        grid_spec=pltpu.PrefetchScalarGridSpec(
            num_scalar_prefetch=2,
            grid=(nseq, n_blocks + 1),
            in_specs=[seq_spec((16 * n_new, width))]
            + [page_spec(r) for r in range(EVEN_PAGES)] + [page_spec(r) for r in range(EVEN_PAGES)]
            + [seq_spec((NEW_PAD, width)), seq_spec((NEW_PAD, width)),
               pl.BlockSpec((1, A_VDIM), lambda s, p, pt, lm: (0, 0))],
            out_specs=seq_spec((n_new, A_V + B_W)),
            scratch_shapes=[pltpu.VMEM((ra, 1), F32), pltpu.VMEM((ra, 1), F32), pltpu.VMEM((ra, A_V), F32),
                            pltpu.VMEM((n_blocks + 1, ra, 1), F32), pltpu.VMEM((n_blocks + 1, ra, 1), F32),
                            pltpu.VMEM((n_blocks + 1, ra, B_W), F32), pltpu.VMEM((nbp, B_W), F32)],
        ),
        out_shape=jax.ShapeDtypeStruct((nseq, n_new, A_V + B_W), BF16),
        compiler_params=_params(("parallel", "arbitrary")),
        name="dec_even",
    )(page_table, lam, qbd, *([cache_k] * EVEN_PAGES), *([cache_v] * EVEN_PAGES), knew, vnew, subln_g)


ODD_PAGES = 8
ODD_CHUNK = ODD_PAGES * PAGE_SIZE


def _dec_select_kernel(pt_ref, qi_ref, wi_ref, *refs, n_new, n_chunks, topk):
    pages = refs[:ODD_PAGES]
    kinew_ref, bias_ref, key_ref, tri_ref = refs[ODD_PAGES:]
    p = pl.program_id(1)
    qi = qi_ref[0]
    wi = wi_ref[0]

    def scores(ki16):
        lg = _dot_nt(qi, ki16)
        return _fold_heads(wi * jnp.maximum(lg, 0.0), IDX_HEADS, n_new)

    @pl.when(p < n_chunks)
    def _():
        for r in range(ODD_PAGES):
            key_ref[p, :, r * PAGE_SIZE:(r + 1) * PAGE_SIZE] = _order_key(scores(pages[r][...].astype(BF16)))

    @pl.when(p == n_chunks)
    def _():
        sc = scores(kinew_ref[0])
        t = lax.broadcasted_iota(jnp.int32, sc.shape, 0)
        j = lax.broadcasted_iota(jnp.int32, sc.shape, 1)
        key_ref[n_chunks] = jnp.full((n_new, ODD_CHUNK), INT_MIN, jnp.int32)
        key_ref[n_chunks, :, 0:PAGE_SIZE] = jnp.where(j <= t, _order_key(sc), INT_MIN)
        qpos = n_chunks * ODD_CHUNK + lax.broadcasted_iota(jnp.int32, (n_new, 1), 0)
        kk = jnp.minimum(topk, qpos + 1).astype(F32)
        _topk_bias(key_ref, bias_ref.at[0], tri_ref, n_chunks + 1, kk, axis=1)


def _dec_select(page_table, qi, wi, cache_ki, layer, kinew):
    nseq, n_pages = page_table.shape
    n_new = qi.shape[1] // IDX_HEADS
    n_chunks = n_pages // ODD_PAGES
    topk = min(DSA_TOPK, (n_pages * PAGE_SIZE + n_new) // 4)

    def page_spec(r):
        return pl.BlockSpec((None, None, PAGE_SIZE, IDX_DIM),
                            lambda s, p, pt: (pt[s, jnp.minimum(p, n_chunks - 1) * ODD_PAGES + r], layer, 0, 0))

    return pl.pallas_call(
        functools.partial(_dec_select_kernel, n_new=n_new, n_chunks=n_chunks, topk=topk),
        grid_spec=pltpu.PrefetchScalarGridSpec(
            num_scalar_prefetch=1,
            grid=(nseq, n_chunks + 1),
            in_specs=[pl.BlockSpec((1, IDX_HEADS * n_new, IDX_DIM), lambda s, p, pt: (s, 0, 0)),
                      pl.BlockSpec((1, IDX_HEADS * n_new, 1), lambda s, p, pt: (s, 0, 0))]
            + [page_spec(r) for r in range(ODD_PAGES)]
            + [pl.BlockSpec((1, PAGE_SIZE, IDX_DIM), lambda s, p, pt: (s, 0, 0))],
            out_specs=pl.BlockSpec((1, n_chunks + 1, n_new, ODD_CHUNK), lambda s, p, pt: (s, 0, 0, 0)),
            scratch_shapes=[pltpu.VMEM((n_chunks + 1, n_new, ODD_CHUNK), jnp.int32),
                            pltpu.VMEM((ODD_CHUNK, ODD_CHUNK), BF16)],
        ),
        out_shape=jax.ShapeDtypeStruct((nseq, n_chunks + 1, n_new, ODD_CHUNK), F32),
        compiler_params=_params(("parallel", "arbitrary")),
        name="dec_select",
    )(page_table, qi, wi, *([cache_ki] * ODD_PAGES), kinew)


def _dec_odd_kernel(pt_ref, qbd_ref, bias_ref, *refs, n_new, n_chunks):
    kpages = refs[:ODD_PAGES]
    vpages = refs[ODD_PAGES:2 * ODD_PAGES]
    knew_ref, vnew_ref, o_ref, m_ref, l_ref, acc_ref = refs[2 * ODD_PAGES:]
    p = pl.program_id(1)
    qbd = qbd_ref[0]

    @pl.when(p == 0)
    def _():
        m_ref[...] = jnp.full_like(m_ref, NEG)
        l_ref[...] = jnp.zeros_like(l_ref)
        acc_ref[...] = jnp.zeros_like(acc_ref)

    def attend(k16, v16, bias):
        s = _dot_nt(qbd, k16) + jnp.tile(bias, (C_HEADS, 1))
        m_prev = m_ref[...]
        m_new = jnp.maximum(m_prev, jnp.max(s, axis=1, keepdims=True))
        alpha = jnp.exp(m_prev - m_new)
        pr = jnp.exp(s - m_new)
        l_ref[...] = alpha * l_ref[...] + jnp.sum(pr, axis=1, keepdims=True)
        acc_ref[...] = alpha * acc_ref[...] + _dot(pr.astype(BF16), v16)
        m_ref[...] = m_new

    @pl.when(p < n_chunks)
    def _():
        k16 = jnp.concatenate([r[...] for r in kpages], axis=0).astype(BF16)
        v16 = jnp.concatenate([r[...] for r in vpages], axis=0).astype(BF16)
        attend(k16, v16, bias_ref[0, 0])

    @pl.when(p == n_chunks)
    def _():
        attend(knew_ref[0], vnew_ref[0], bias_ref[0, 0, :, 0:NEW_PAD])
        r = acc_ref[...] / l_ref[...]
        group = C_HEADS // C_KV_HEADS
        for h in range(C_HEADS):
            g = h // group
            o_ref[0, :, h * HEAD_DIM:(h + 1) * HEAD_DIM] = (
                r[h * n_new:(h + 1) * n_new, g * HEAD_DIM:(g + 1) * HEAD_DIM].astype(BF16))


def _dec_odd(page_table, qbd, bias, cache_k, cache_v, layer, knew, vnew):
    nseq, n_pages = page_table.shape
    n_new = qbd.shape[1] // C_HEADS
    n_chunks = n_pages // ODD_PAGES

    def page_spec(r):
        return pl.BlockSpec((None, None, PAGE_SIZE, C_KV),
                            lambda s, p, pt: (pt[s, jnp.minimum(p, n_chunks - 1) * ODD_PAGES + r], layer, 0, 0))

    seq_spec = lambda shape: pl.BlockSpec((1,) + shape, lambda s, p, pt: (s, 0, 0))
    rows = C_HEADS * n_new
    return pl.pallas_call(
        functools.partial(_dec_odd_kernel, n_new=n_new, n_chunks=n_chunks),
        grid_spec=pltpu.PrefetchScalarGridSpec(
            num_scalar_prefetch=1,
            grid=(nseq, n_chunks + 1),
            in_specs=[seq_spec((rows, C_KV)),
                      pl.BlockSpec((1, 1, n_new, ODD_CHUNK), lambda s, p, pt: (s, p, 0, 0))]
            + [page_spec(r) for r in range(ODD_PAGES)] + [page_spec(r) for r in range(ODD_PAGES)]
            + [seq_spec((NEW_PAD, C_KV)), seq_spec((NEW_PAD, C_KV))],
            out_specs=seq_spec((n_new, C_Q)),
            scratch_shapes=[pltpu.VMEM((rows, 1), F32), pltpu.VMEM((rows, 1), F32), pltpu.VMEM((rows, C_KV), F32)],
        ),
        out_shape=jax.ShapeDtypeStruct((nseq, n_new, C_Q), BF16),
        compiler_params=_params(("parallel", "arbitrary")),
        name="dec_odd",
    )(page_table, qbd, bias, *([cache_k] * ODD_PAGES), *([cache_v] * ODD_PAGES), knew, vnew)


ROW_TILE = 1024
FFN_TF = 256


def _pad_rows(a, rows):
    return jnp.pad(a, ((0, 0), (0, rows - a.shape[1]), (0, 0)))


def _gqa_rows(q):
    s, t, nh, d = q.shape
    group = nh // C_KV_HEADS
    keep = (jnp.arange(nh)[:, None] // group) == jnp.arange(C_KV_HEADS)[None, :]
    qh = jnp.swapaxes(q, 1, 2)
    out = jnp.where(keep[None, :, None, :, None], qh[:, :, :, None, :], 0)
    return out.reshape(s, nh * t, C_KV_HEADS * d).astype(q.dtype)


def kernel(x_prompt, x_sample, cache_k_even, cache_v_even, cache_k_odd, cache_v_odd, cache_kidx_odd, state_conv,
           page_table, g_mix, g_ffn, g_final, w_in_even, w_out_even, lam_even, subln_even, w_in_odd, w_out_odd,
           w_gate, w_up, conv_w, conv_b, w_down):
    bsz, seq, d = x_prompt.shape
    nseq, n_new, _ = x_sample.shape
    depth = g_mix.shape[0]
    ff = w_gate.shape[-1]
    past_len = page_table.shape[1] * PAGE_SIZE
    bf = lambda a: a.astype(BF16)

    cos_p, sin_p = _rope_tables(jnp.arange(seq, dtype=jnp.int32))
    cos_s, sin_s = _rope_tables(jnp.tile(past_len + jnp.arange(n_new, dtype=jnp.int32), nseq))
    ecols, ocols = _even_cols(), _odd_cols()
    qb0 = 2 * A_QK + A_V
    tm = min(ROW_TILE, seq)
    ms = nseq * n_new

    hp = x_prompt.reshape(bsz * seq, d)
    hs = x_sample.reshape(ms, d)
    ke_p, ve_p, ko_p, vo_p, kio_p, cs_p = [], [], [], [], [], []
    ke_s, ve_s, ko_s, vo_s, kio_s, cs_s = [], [], [], [], [], []
    for layer in range(depth):
        e = layer // 2
        g = g_mix[layer][None]
        if layer % 2 == 0:
            w16, wo = bf(w_in_even[e]), bf(w_out_even[e])
            lam_init = 0.8 - 0.6 * math.exp(-0.3 * layer)
            lp = lam_even[e].astype(F32)
            lam = (jnp.exp(jnp.sum(lp[0] * lp[1])) - jnp.exp(jnp.sum(lp[2] * lp[3])) + lam_init).reshape(1)
            sg = subln_even[e][None]
            out_scale = 1.0 - lam_init

            p32, p16 = _inproj(hp, g, w16, cos_p, sin_p, *ecols, tm=tm)
            p32, p16 = p32.reshape(bsz, seq, -1), p16.reshape(bsz, seq, -1)
            pair = jnp.arange(2 * A_HEADS) % 2
            qat = _heads_t(p16[..., :A_QK].reshape(bsz, seq, 2 * A_HEADS, HEAD_DIM), pair)
            vat = jnp.swapaxes(p16[..., 2 * A_QK:qb0], 1, 2)
            oat = _diff_attn(lam, qat, p16, vat, sg.reshape(A_VDIM, 1), out_scale, tq=256, tk=512)
            kmean = _blockmean(p32, (qb0 + B_W) // B_W, B_W, nb_step=min(8, seq // MOBA_BLOCK))
            qbt = _heads_t(p16[..., qb0:qb0 + B_W].reshape(bsz, seq, B_HEADS, HEAD_DIM), pair)
            q32t = jnp.swapaxes(p32[..., qb0:qb0 + B_W], 1, 2)
            vbt = jnp.swapaxes(p16[..., qb0 + 2 * B_W:], 1, 2)
            obt = _moba_attn(qbt, q32t, p16, vbt, kmean, sub=2)
            o = jnp.swapaxes(jnp.concatenate([oat, obt], axis=1), 1, 2)
            hp = _outproj([o.reshape(bsz * seq, -1)], [wo], hp, tm=tm)
            ke_p.append(jnp.concatenate([p32[..., A_QK:2 * A_QK], p32[..., qb0 + B_W:qb0 + 2 * B_W]], -1))
            ve_p.append(jnp.concatenate([p32[..., 2 * A_QK:qb0], p32[..., qb0 + 2 * B_W:]], -1))

            s32, s16 = _inproj(hs, g, w16, cos_s, sin_s, *ecols, tm=ms)
            s32, s16 = s32.reshape(nseq, n_new, -1), s16.reshape(nseq, n_new, -1)
            qbd = _block_diag_rows(jnp.concatenate([s16[..., :A_QK], s16[..., qb0:qb0 + B_W]], -1), 16, HEAD_DIM)
            knew = jnp.concatenate([s16[..., A_QK:2 * A_QK], s16[..., qb0 + B_W:qb0 + 2 * B_W]], -1)
            vnew = jnp.concatenate([s16[..., 2 * A_QK:qb0], s16[..., qb0 + 2 * B_W:]], -1)
            o = _dec_even(page_table, lam, qbd, cache_k_even, cache_v_even, e,
                          _pad_rows(knew, NEW_PAD), _pad_rows(vnew, NEW_PAD), sg, out_scale)
            hs = _outproj([o.reshape(ms, -1)], [wo], hs, tm=ms)
            ke_s.append(jnp.concatenate([s32[..., A_QK:2 * A_QK], s32[..., qb0 + B_W:qb0 + 2 * B_W]], -1))
            ve_s.append(jnp.concatenate([s32[..., 2 * A_QK:qb0], s32[..., qb0 + 2 * B_W:]], -1))
        else:
            w16 = bf(jnp.pad(w_in_odd[e], ((0, 0), (0, ODD_NPAD - ODD_N))))
            wo = bf(w_out_odd[e])
            kc, vc, kic, wic = C_Q, C_Q + C_KV, ODD_KI_COL, ODD_KI_COL + IDX_DIM

            p32, p16 = _inproj(hp, g, w16, cos_p, sin_p, *ocols, tm=tm)
            p32, p16 = p32.reshape(bsz, seq, -1), p16.reshape(bsz, seq, -1)
            tc = min(512, seq)
            qt = _heads_t(p16[..., :C_Q].reshape(bsz, seq, C_HEADS, HEAD_DIM),
                          (jnp.arange(C_HEADS) // (C_HEADS // C_KV_HEADS)) % 2)
            qit = _heads_t(p16[..., vc + C_KV:kic].reshape(bsz, seq, IDX_HEADS, IDX_DIM), jnp.zeros(IDX_HEADS, jnp.int32))
            wit = jnp.swapaxes(p32[..., wic:wic + IDX_HEADS], 1, 2)
            vt4 = jnp.swapaxes(p16[..., vc:vc + C_KV].reshape(bsz, seq // tc, tc, C_KV), 2, 3)
            ot = _dsa_attn(qt, qit, wit, p16, vt4, tq=256, tc=tc)
            hp = _outproj([jnp.swapaxes(ot, 1, 2).reshape(bsz * seq, -1)], [wo], hp, tm=tm)
            ko_p.append(p32[..., kc:kc + C_KV])
            vo_p.append(p32[..., vc:vc + C_KV])
            kio_p.append(p32[..., kic:kic + IDX_DIM])

            s32, s16 = _inproj(hs, g, w16, cos_s, sin_s, *ocols, tm=ms)
            s32, s16 = s32.reshape(nseq, n_new, -1), s16.reshape(nseq, n_new, -1)
            qi = s16[..., vc + C_KV:kic].reshape(nseq, n_new, IDX_HEADS, IDX_DIM)
            qi = jnp.swapaxes(qi, 1, 2).reshape(nseq, IDX_HEADS * n_new, IDX_DIM)
            wi = jnp.swapaxes(s32[..., wic:wic + IDX_HEADS], 1, 2).reshape(nseq, IDX_HEADS * n_new, 1)
            bias = _dec_select(page_table, qi, wi, cache_kidx_odd, e, _pad_rows(s16[..., kic:kic + IDX_DIM], PAGE_SIZE))
            qbd = _gqa_rows(s16[..., :C_Q].reshape(nseq, n_new, C_HEADS, HEAD_DIM))
            o = _dec_odd(page_table, qbd, bias, cache_k_odd, cache_v_odd, e,
                         _pad_rows(s16[..., kc:kc + C_KV], NEW_PAD), _pad_rows(s16[..., vc:vc + C_KV], NEW_PAD))
            hs = _outproj([o.reshape(ms, -1)], [wo], hs, tm=ms)
            ko_s.append(s32[..., kc:kc + C_KV])
            vo_s.append(s32[..., vc:vc + C_KV])
            kio_s.append(s32[..., kic:kic + IDX_DIM])

        last = layer == depth - 1
        ffn_w = (g_ffn[layer][None], bf(w_gate[layer]), bf(w_up[layer]), conv_w[layer], conv_b[layer][None],
                 bf(w_down[layer]), g_final[None])
        hp, tail = _ffn(hp, *ffn_w, seq_len=seq, tm=tm, tf=FFN_TF, final_norm=last)
        cs_p.append(tail.reshape(bsz, seq // tm, 8, ff)[:, -1, 8 - (CONV_W - 1):])
        hs, gout = _ffn_s(hs, state_conv[layer], *ffn_w, seq_len=n_new, tf=FFN_TF, final_norm=last)
        cs_s.append(gout.reshape(nseq, n_new, ff)[:, n_new - (CONV_W - 1):])

    def to_pages(rows):
        r = jnp.stack(rows, 1)
        b, nl, s, w = r.shape
        return r.reshape(b, nl, s // PAGE_SIZE, PAGE_SIZE, w).transpose(0, 2, 1, 3, 4)

    return (hp.reshape(bsz, seq, d), hs.reshape(nseq, n_new, d),
            to_pages(ke_p), to_pages(ve_p), to_pages(ko_p), to_pages(vo_p), to_pages(kio_p), jnp.stack(cs_p, 0),
            jnp.stack(ke_s, 1), jnp.stack(ve_s, 1), jnp.stack(ko_s, 1), jnp.stack(vo_s, 1), jnp.stack(kio_s, 1),
            jnp.stack(cs_s, 0))
```

```python
import functools
import math

import jax
import jax.numpy as jnp
import numpy as np
from jax import lax
from jax.experimental import pallas as pl
from jax.experimental.pallas import tpu as pltpu

F32 = jnp.float32
BF16 = jnp.bfloat16

HEAD_DIM = 64
A_HEADS = 4
A_VDIM = 2 * HEAD_DIM
B_HEADS = 8
MOBA_BLOCK = 256
MOBA_TOPK = 3
C_HEADS = 16
C_KV_HEADS = 4
IDX_HEADS = 8
IDX_DIM = 64
DSA_TOPK = 256
CONV_W = 3
ROPE_THETA = 10000.0
EPS = 1e-6
PAGE_SIZE = 128
A_QK = A_HEADS * 2 * HEAD_DIM
A_V = A_HEADS * A_VDIM
B_W = B_HEADS * HEAD_DIM
C_Q = C_HEADS * HEAD_DIM
C_KV = C_KV_HEADS * HEAD_DIM
IDX_Q = IDX_HEADS * IDX_DIM
IDX_W_SCALE = IDX_Q ** -0.5
ATTN_SCALE = HEAD_DIM ** -0.5
Q_SCALE = ATTN_SCALE * math.log2(math.e)

LANES = 128
NEG = -1e30
VMEM_LIMIT = 56 * 1024 * 1024

EVEN_N = 2 * A_QK + A_V + 3 * B_W
ODD_N = C_Q + 2 * C_KV + IDX_Q + IDX_DIM + IDX_HEADS
ODD_NPAD = 2304
ODD_KI_COL = C_Q + 2 * C_KV + IDX_Q
PROJ_TN = 768


def _params(sem, vmem=VMEM_LIMIT):
    return pltpu.CompilerParams(dimension_semantics=sem, vmem_limit_bytes=vmem)


def _dot_nt(a, b):
    return lax.dot_general(a, b, (((1,), (1,)), ((), ())), preferred_element_type=F32)


def _dot(a, b):
    return jnp.dot(a, b, preferred_element_type=F32)


def _rms(x, g):
    return x * lax.rsqrt(jnp.mean(x * x, axis=-1, keepdims=True) + EPS) * g


def _inproj_kernel(x_ref, g_ref, w_ref, cos_ref, sin_ref, rmask_ref, s32_ref, s16_ref,
                   o32_ref, o16_ref, xn_ref):
    @pl.when(pl.program_id(1) == 0)
    def _():
        xn_ref[...] = _rms(x_ref[...], g_ref[...]).astype(BF16)

    y = _dot(xn_ref[...], w_ref[...])
    tn = y.shape[1]
    reps = tn // LANES
    cos = jnp.tile(cos_ref[...], (1, reps))
    sin = jnp.tile(sin_ref[...], (1, reps))
    lane = lax.broadcasted_iota(jnp.int32, y.shape, 1)
    first_half = (lane % HEAD_DIM) < (HEAD_DIM // 2)
    partner = jnp.where(first_half, pltpu.roll(y, tn - HEAD_DIM // 2, 1), pltpu.roll(y, HEAD_DIM // 2, 1))
    roped = y * cos + partner * sin
    out = jnp.where(rmask_ref[...] > 0.5, roped, y) * s32_ref[...]
    o32_ref[...] = out
    o16_ref[...] = (out * s16_ref[...]).astype(BF16)


def _inproj(x, g, w16, cos, sin, rmask, s32, s16, tm):
    m, d = x.shape
    n = w16.shape[1]
    tn = PROJ_TN
    tblocks = cos.shape[0] // tm
    return pl.pallas_call(
        _inproj_kernel,
        grid=(m // tm, n // tn),
        in_specs=[
            pl.BlockSpec((tm, d), lambda i, j: (i, 0)),
            pl.BlockSpec((1, d), lambda i, j: (0, 0)),
            pl.BlockSpec((d, tn), lambda i, j: (0, j)),
            pl.BlockSpec((tm, LANES), lambda i, j: (i % tblocks, 0)),
            pl.BlockSpec((tm, LANES), lambda i, j: (i % tblocks, 0)),
            pl.BlockSpec((1, tn), lambda i, j: (0, j)),
            pl.BlockSpec((1, tn), lambda i, j: (0, j)),
            pl.BlockSpec((1, tn), lambda i, j: (0, j)),
        ],
        out_specs=[
            pl.BlockSpec((tm, tn), lambda i, j: (i, j)),
            pl.BlockSpec((tm, tn), lambda i, j: (i, j)),
        ],
        out_shape=[jax.ShapeDtypeStruct((m, n), F32), jax.ShapeDtypeStruct((m, n), BF16)],
        scratch_shapes=[pltpu.VMEM((tm, d), BF16)],
        compiler_params=_params(("parallel", "arbitrary")),
        name="inproj",
    )(x, g, w16, cos, sin, rmask, s32, s16)


def _rope_tables(pos):
    half = HEAD_DIM // 2
    inv = ROPE_THETA ** (-jnp.arange(half, dtype=F32) * 2.0 / HEAD_DIM)
    ang = pos.astype(F32)[:, None] * inv[None, :]
    cos = jnp.tile(jnp.cos(ang), (1, LANES // half))
    sin = jnp.tile(jnp.concatenate([-jnp.sin(ang), jnp.sin(ang)], axis=1), (1, LANES // HEAD_DIM))
    return cos, sin


def _col_rows(n, rope_ranges, scale32, scale16):
    cols = jnp.arange(n)
    rmask = jnp.zeros((n,), F32)
    for lo, hi in rope_ranges:
        rmask = jnp.where((cols >= lo) & (cols < hi), 1.0, rmask)
    s32 = jnp.ones((n,), F32)
    for lo, hi, v in scale32:
        s32 = jnp.where((cols >= lo) & (cols < hi), v, s32)
    s16 = jnp.ones((n,), F32)
    for lo, hi, v in scale16:
        s16 = jnp.where((cols >= lo) & (cols < hi), v, s16)
    return rmask[None], s32[None], s16[None]


def _even_cols():
    o = A_QK + A_QK + A_V
    return _col_rows(EVEN_N, [(0, 2 * A_QK), (o, o + 2 * B_W)], [],
                     [(0, A_QK, Q_SCALE), (o, o + B_W, Q_SCALE)])


def _odd_cols():
    return _col_rows(ODD_NPAD, [(0, C_Q + C_KV), (C_Q + 2 * C_KV, ODD_KI_COL + IDX_DIM)],
                     [(ODD_KI_COL + IDX_DIM, ODD_N, IDX_W_SCALE)], [(0, C_Q, Q_SCALE)])


def _outproj_kernel(*refs, n_in):
    xs, ws, h_ref, o_ref = refs[:n_in], refs[n_in:2 * n_in], refs[2 * n_in], refs[2 * n_in + 1]
    acc = h_ref[...]
    for x_ref, w_ref in zip(xs, ws):
        acc = acc + _dot(x_ref[...], w_ref[...])
    o_ref[...] = acc


def _outproj(xs, ws, h, tm):
    m, d = h.shape
    n_in = len(xs)
    in_specs = [pl.BlockSpec((tm, x.shape[1]), lambda i: (i, 0)) for x in xs]
    in_specs += [pl.BlockSpec(w.shape, lambda i: (0, 0)) for w in ws]
    in_specs += [pl.BlockSpec((tm, d), lambda i: (i, 0))]
    return pl.pallas_call(
        functools.partial(_outproj_kernel, n_in=n_in),
        grid=(m // tm,),
        in_specs=in_specs,
        out_specs=pl.BlockSpec((tm, d), lambda i: (i, 0)),
        out_shape=jax.ShapeDtypeStruct((m, d), F32),
        compiler_params=_params(("parallel",)),
        name="outproj",
    )(*xs, *ws, h)


FFN_HALO = 16


def _ffn_kernel(h_ref, halo_ref, g_ref, wg_ref, wu_ref, cw_ref, cb_ref, wd_ref, gf_ref,
                o_ref, tail_ref, xn_ref, gext_ref, acc_ref, *, tiles_per_seq, final_norm):
    i, f = pl.program_id(0), pl.program_id(1)
    tm = h_ref.shape[0]

    @pl.when(f == 0)
    def _():
        xn_ref[0:FFN_HALO, :] = _rms(halo_ref[...], g_ref[...]).astype(BF16)
        xn_ref[FFN_HALO:, :] = _rms(h_ref[...], g_ref[...]).astype(BF16)
        acc_ref[...] = jnp.zeros_like(acc_ref)

    xn = xn_ref[...]
    gext = _dot(xn, wg_ref[...])
    first = (i % tiles_per_seq) == 0
    row = lax.broadcasted_iota(jnp.int32, gext.shape, 0)
    gext_ref[...] = jnp.where(jnp.logical_and(first, row < FFN_HALO), 0.0, gext)
    g = gext_ref[FFN_HALO:, :]
    p1 = gext_ref[pl.ds(FFN_HALO - 1, tm), :]
    p2 = gext_ref[pl.ds(FFN_HALO - 2, tm), :]
    cw = cw_ref[...]
    c = cb_ref[...] + p2 * cw[0:1] + p1 * cw[1:2] + g * cw[2:3]
    u = _dot(xn[FFN_HALO:], wu_ref[...])
    a = (c * jax.nn.sigmoid(c) * u).astype(BF16)
    acc_ref[...] += _dot(a, wd_ref[...])
    tail_ref[0] = g[tm - 8:, :]

    @pl.when(f == pl.num_programs(1) - 1)
    def _():
        out = h_ref[...] + acc_ref[...]
        if final_norm:
            out = _rms(out, gf_ref[...])
        o_ref[...] = out


def _ffn(h, g, wg, wu, cw, cb, wd, gf, seq_len, tm, tf, final_norm):
    m, d = h.shape
    ff = wg.shape[1]
    nt = m // tm
    hb = tm // FFN_HALO
    out, tail = pl.pallas_call(
        functools.partial(_ffn_kernel, tiles_per_seq=seq_len // tm, final_norm=final_norm),
        grid=(nt, ff // tf),
        in_specs=[
            pl.BlockSpec((tm, d), lambda i, f: (i, 0)),
            pl.BlockSpec((FFN_HALO, d), lambda i, f: (jnp.maximum(i * hb - 1, 0), 0)),
            pl.BlockSpec((1, d), lambda i, f: (0, 0)),
            pl.BlockSpec((d, tf), lambda i, f: (0, f)),
            pl.BlockSpec((d, tf), lambda i, f: (0, f)),
            pl.BlockSpec((CONV_W, tf), lambda i, f: (0, f)),
            pl.BlockSpec((1, tf), lambda i, f: (0, f)),
            pl.BlockSpec((tf, d), lambda i, f: (f, 0)),
            pl.BlockSpec((1, d), lambda i, f: (0, 0)),
        ],
        out_specs=[
            pl.BlockSpec((tm, d), lambda i, f: (i, 0)),
            pl.BlockSpec((1, 8, tf), lambda i, f: (i, 0, f)),
        ],
        out_shape=[jax.ShapeDtypeStruct((m, d), F32), jax.ShapeDtypeStruct((nt, 8, ff), F32)],
        scratch_shapes=[pltpu.VMEM((FFN_HALO + tm, d), BF16), pltpu.VMEM((FFN_HALO + tm, tf), F32),
                        pltpu.VMEM((tm, d), F32)],
        compiler_params=_params(("parallel", "arbitrary")),
        name="ffn",
    )(h, h, g, wg, wu, cw, cb, wd, gf)
    return out, tail


def _ffn_s_kernel(h_ref, b1_ref, b2_ref, g_ref, wg_ref, wu_ref, cw_ref, cb_ref, wd_ref, gf_ref,
                  o_ref, gout_ref, xn_ref, acc_ref, *, seq_len, final_norm):
    f = pl.program_id(0)

    @pl.when(f == 0)
    def _():
        xn_ref[...] = _rms(h_ref[...], g_ref[...]).astype(BF16)
        acc_ref[...] = jnp.zeros_like(acc_ref)

    xn = xn_ref[...]
    g = _dot(xn, wg_ref[...])
    t = lax.broadcasted_iota(jnp.int32, g.shape, 0) % seq_len
    p1 = jnp.where(t < 1, b1_ref[...], pltpu.roll(g, 1, 0))
    p2 = jnp.where(t < 2, b2_ref[...], pltpu.roll(g, 2, 0))
    cw = cw_ref[...]
    c = cb_ref[...] + p2 * cw[0:1] + p1 * cw[1:2] + g * cw[2:3]
    u = _dot(xn, wu_ref[...])
    a = (c * jax.nn.sigmoid(c) * u).astype(BF16)
    acc_ref[...] += _dot(a, wd_ref[...])
    gout_ref[...] = g

    @pl.when(f == pl.num_programs(0) - 1)
    def _():
        out = h_ref[...] + acc_ref[...]
        if final_norm:
            out = _rms(out, gf_ref[...])
        o_ref[...] = out


def _ffn_s(h, buf, g, wg, wu, cw, cb, wd, gf, seq_len, tf, final_norm):
    m, d = h.shape
    ff = wg.shape[1]
    nseq = m // seq_len
    zeros = jnp.zeros((nseq, seq_len - 2, ff), F32)
    b1 = jnp.concatenate([buf[:, 1:2], jnp.zeros((nseq, 1, ff), F32), zeros], axis=1).reshape(m, ff)
    b2 = jnp.concatenate([buf[:, 0:1], buf[:, 1:2], zeros], axis=1).reshape(m, ff)
    out, gout = pl.pallas_call(
        functools.partial(_ffn_s_kernel, seq_len=seq_len, final_norm=final_norm),
        grid=(ff // tf,),
        in_specs=[
            pl.BlockSpec((m, d), lambda f: (0, 0)),
            pl.BlockSpec((m, tf), lambda f: (0, f)),
            pl.BlockSpec((m, tf), lambda f: (0, f)),
            pl.BlockSpec((1, d), lambda f: (0, 0)),
            pl.BlockSpec((d, tf), lambda f: (0, f)),
            pl.BlockSpec((d, tf), lambda f: (0, f)),
            pl.BlockSpec((CONV_W, tf), lambda f: (0, f)),
            pl.BlockSpec((1, tf), lambda f: (0, f)),
            pl.BlockSpec((tf, d), lambda f: (f, 0)),
            pl.BlockSpec((1, d), lambda f: (0, 0)),
        ],
        out_specs=[
            pl.BlockSpec((m, d), lambda f: (0, 0)),
            pl.BlockSpec((m, tf), lambda f: (0, f)),
        ],
        out_shape=[jax.ShapeDtypeStruct((m, d), F32), jax.ShapeDtypeStruct((m, ff), F32)],
        scratch_shapes=[pltpu.VMEM((m, d), BF16), pltpu.VMEM((m, d), F32)],
        compiler_params=_params(("arbitrary",)),
        name="ffn_sample",
    )(h, b1, b2, g, wg, wu, cw, cb, wd, gf)
    return out, gout


SLAB = 2 * HEAD_DIM


def _heads_t(q, slot):
    b, l, nh, dd = q.shape
    keep = jnp.asarray(slot)[:, None] == jnp.arange(2)[None, :]
    qt = jnp.transpose(q, (0, 2, 3, 1))
    out = jnp.where(keep[None, :, :, None, None], qt[:, :, None, :, :], jnp.zeros((), q.dtype))
    return out.reshape(b, nh * SLAB, l)


REDUCE_WAYS = 8


def _rows_fold(x, reduce_fn):
    r, n = x.shape
    ways = REDUCE_WAYS if r % (8 * REDUCE_WAYS) == 0 else 1
    part = reduce_fn(x.reshape(r // (8 * ways), ways, 8, n), axis=0)
    return reduce_fn(part, axis=0)


def _flash_update_t(sts, vts, m_ref, l_ref, acc_ref, idx):
    m_prev = m_ref[idx]
    m_new = functools.reduce(jnp.maximum, [jnp.max(s, axis=0, keepdims=True) for s in sts], m_prev)
    alpha = jnp.exp2(m_prev - m_new)
    ps = [jnp.exp2(s - m_new) for s in sts]
    l_ref[idx] = alpha * l_ref[idx] + functools.reduce(jnp.add, [jnp.sum(p, axis=0, keepdims=True) for p in ps])
    acc = alpha * acc_ref[idx]
    for vt, p in zip(vts, ps):
        acc = acc + _dot(vt, p.astype(BF16))
    acc_ref[idx] = acc
    m_ref[idx] = m_new


def _flash_init(m_ref, l_ref, acc_ref):
    m_ref[...] = jnp.full_like(m_ref, NEG)
    l_ref[...] = jnp.zeros_like(l_ref)
    acc_ref[...] = jnp.zeros_like(acc_ref)


def _diff_kernel(lam_ref, qt_ref, k_ref, vt_ref, g_ref, o_ref, s_ref, m_ref, l_ref, acc_ref, *, tq, tk, out_scale):
    i, j = pl.program_id(1), pl.program_id(2)
    jlast = ((i + 1) * tq - 1) // tk

    @pl.when(j == 0)
    def _():
        _flash_init(m_ref, l_ref, acc_ref)

    def step(masked):
        if masked:
            kpos = j * tk + lax.broadcasted_iota(jnp.int32, (tk, 2 * tq), 0)
            qpos = i * tq + lax.broadcasted_iota(jnp.int32, (tk, 2 * tq), 1) % tq
            ok = kpos <= qpos
        def scores(h):
            q2 = jnp.concatenate([qt_ref[0, m * SLAB:(m + 1) * SLAB, :] for m in (2 * h, 2 * h + 1)], axis=1)
            s = _dot(k_ref[0, :, h * SLAB:(h + 1) * SLAB], q2)
            s_ref[h % 2] = jnp.where(ok, s, NEG) if masked else s

        scores(0)
        for h in range(A_HEADS):
            if h + 1 < A_HEADS:
                scores(h + 1)
            _flash_update_t([s_ref[h % 2]], [vt_ref[0, h * A_VDIM:(h + 1) * A_VDIM, :]], m_ref, l_ref, acc_ref, h)

    crosses = (j + 1) * tk - 1 > i * tq

    @pl.when(jnp.logical_and(j <= jlast, crosses))
    def _():
        step(True)

    @pl.when(jnp.logical_and(j <= jlast, jnp.logical_not(crosses)))
    def _():
        step(False)

    @pl.when(j == jlast)
    def _():
        lam = lam_ref[0]
        for h in range(A_HEADS):
            r = acc_ref[h] / l_ref[h]
            o = r[:, :tq] - lam * r[:, tq:]
            o = o * lax.rsqrt(jnp.mean(o * o, axis=0, keepdims=True) + EPS) * g_ref[...] * out_scale
            o_ref[0, h * A_VDIM:(h + 1) * A_VDIM, :] = o.astype(BF16)


def _diff_attn(lam, qt, qkv16, vt, subln_col, out_scale, tq, tk):
    b, l, _ = qkv16.shape
    nq, nk = l // tq, l // tk
    jmax = lambda i, j: jnp.minimum(j, ((i + 1) * tq - 1) // tk)
    return pl.pallas_call(
        functools.partial(_diff_kernel, tq=tq, tk=tk, out_scale=out_scale),
        grid_spec=pltpu.PrefetchScalarGridSpec(
            num_scalar_prefetch=1,
            grid=(b, nq, nk),
            in_specs=[
                pl.BlockSpec((1, 2 * A_HEADS * SLAB, tq), lambda bi, i, j, lam_ref: (bi, 0, i)),
                pl.BlockSpec((1, tk, A_QK), lambda bi, i, j, lam_ref: (bi, jmax(i, j), 1)),
                pl.BlockSpec((1, A_V, tk), lambda bi, i, j, lam_ref: (bi, 0, jmax(i, j))),
                pl.BlockSpec((A_VDIM, 1), lambda bi, i, j, lam_ref: (0, 0)),
            ],
            out_specs=pl.BlockSpec((1, A_V, tq), lambda bi, i, j, lam_ref: (bi, 0, i)),
            scratch_shapes=[pltpu.VMEM((2, tk, 2 * tq), F32),
                            pltpu.VMEM((A_HEADS, 1, 2 * tq), F32), pltpu.VMEM((A_HEADS, 1, 2 * tq), F32),
                            pltpu.VMEM((A_HEADS, A_VDIM, 2 * tq), F32)],
        ),
        out_shape=jax.ShapeDtypeStruct((b, A_V, l), BF16),
        compiler_params=_params(("parallel", "parallel", "arbitrary")),
        name="diff_attn",
    )(lam, qt, qkv16, vt, subln_col)


def _blockmean_kernel(k_ref, o_ref):
    nb = o_ref.shape[1]
    for n in range(nb):
        o_ref[0, n:n + 1, :] = jnp.mean(k_ref[0, n * MOBA_BLOCK:(n + 1) * MOBA_BLOCK, :], axis=0, keepdims=True)


def _blockmean(x32, col_block, width, nb_step):
    b, l, _ = x32.shape
    nb = l // MOBA_BLOCK
    return pl.pallas_call(
        _blockmean_kernel,
        grid=(b, nb // nb_step),
        in_specs=[pl.BlockSpec((1, nb_step * MOBA_BLOCK, width), lambda bi, i: (bi, i, col_block))],
        out_specs=pl.BlockSpec((1, nb_step, width), lambda bi, i: (bi, i, 0)),
        out_shape=jax.ShapeDtypeStruct((b, nb, width), F32),
        compiler_params=_params(("parallel", "parallel")),
        name="blockmean",
    )(x32)


def _top3_select(gate, n_valid, axis=1):
    nb = gate.shape[axis]
    idx = lax.broadcasted_iota(jnp.int32, gate.shape, axis)
    sel = jnp.zeros(gate.shape, F32)
    for r in range(MOBA_TOPK):
        mx = jnp.max(gate, axis=axis, keepdims=True)
        first = jnp.min(jnp.where(gate == mx, idx, nb), axis=axis, keepdims=True)
        pick = idx == first
        sel = jnp.where(jnp.logical_and(pick, r < n_valid), 1.0, sel)
        gate = jnp.where(pick, -jnp.inf, gate)
    return sel


def _dot_nt_f32(a, b):
    return lax.dot_general(a, b, (((1,), (1,)), ((), ())), preferred_element_type=F32,
                           precision=lax.Precision.HIGHEST)


def _dot_f32(a, b):
    return jnp.dot(a, b, preferred_element_type=F32, precision=lax.Precision.HIGHEST)


def _moba_kernel(qt_ref, q32t_ref, k_ref, vt_ref, km_ref, o_ref, sel_ref, s_ref, m_ref, l_ref, acc_ref, *, sub):
    i, j = pl.program_id(1), pl.program_id(2)
    tq = MOBA_BLOCK
    nb = km_ref.shape[1]
    jlast = i // sub

    @pl.when(j == 0)
    def _():
        _flash_init(m_ref, l_ref, acc_ref)
        blk = lax.broadcasted_iota(jnp.int32, (nb, tq), 0)
        for h in range(B_HEADS):
            rows = slice(h * HEAD_DIM, (h + 1) * HEAD_DIM)
            gate = _dot_f32(km_ref[0, :, rows], q32t_ref[0, rows, :])
            sel_ref[h] = _top3_select(jnp.where(blk < i, gate, -jnp.inf), i, axis=0)

    def sub_block(r, own):
        n = j * sub + r
        keys = slice(r * MOBA_BLOCK, (r + 1) * MOBA_BLOCK)
        if own:
            ok = (lax.broadcasted_iota(jnp.int32, (MOBA_BLOCK, 2 * tq), 0)
                  <= lax.broadcasted_iota(jnp.int32, (MOBA_BLOCK, 2 * tq), 1) % tq)
        def scores(p):
            q2 = jnp.concatenate([qt_ref[0, h * SLAB:(h + 1) * SLAB, :] for h in (2 * p, 2 * p + 1)], axis=1)
            s = _dot(k_ref[0, keys, p * SLAB:(p + 1) * SLAB], q2)
            if own:
                chosen = ok
            else:
                chosen = jnp.concatenate([sel_ref[h, pl.ds(n, 1), :] for h in (2 * p, 2 * p + 1)], axis=1) > 0.5
            s_ref[p % 2] = jnp.where(chosen, s, NEG)

        scores(0)
        for p in range(B_HEADS // 2):
            if p + 1 < B_HEADS // 2:
                scores(p + 1)
            _flash_update_t([s_ref[p % 2]], [vt_ref[0, p * SLAB:(p + 1) * SLAB, keys]], m_ref, l_ref, acc_ref, p)

    for r in range(sub):
        n = j * sub + r

        @pl.when(jnp.logical_and(j <= jlast, n < i))
        def _():
            sub_block(r, False)

        @pl.when(n == i)
        def _():
            sub_block(r, True)

    @pl.when(j == jlast)
    def _():
        for h in range(B_HEADS):
            rows = slice((h % 2) * HEAD_DIM, (h % 2 + 1) * HEAD_DIM)
            cols = slice((h % 2) * tq, (h % 2 + 1) * tq)
            o_ref[0, h * HEAD_DIM:(h + 1) * HEAD_DIM, :] = (acc_ref[h // 2, rows, cols] / l_ref[h // 2, :, cols]).astype(BF16)


def _moba_attn(qt, q32t, qkv16, vt, kmean, sub):
    b, l, _ = qkv16.shape
    nb = l // MOBA_BLOCK
    tk = sub * MOBA_BLOCK
    kcol = (2 * A_QK + A_V + B_W) // B_W
    jmax = lambda i, j: jnp.minimum(j, i // sub)
    return pl.pallas_call(
        functools.partial(_moba_kernel, sub=sub),
        grid=(b, nb, l // tk),
        in_specs=[
            pl.BlockSpec((1, B_HEADS * SLAB, MOBA_BLOCK), lambda bi, i, j: (bi, 0, i)),
            pl.BlockSpec((1, B_W, MOBA_BLOCK), lambda bi, i, j: (bi, 0, i)),
            pl.BlockSpec((1, tk, B_W), lambda bi, i, j: (bi, jmax(i, j), kcol)),
            pl.BlockSpec((1, B_W, tk), lambda bi, i, j: (bi, 0, jmax(i, j))),
            pl.BlockSpec((1, nb, B_W), lambda bi, i, j: (bi, 0, 0)),
        ],
        out_specs=pl.BlockSpec((1, B_W, MOBA_BLOCK), lambda bi, i, j: (bi, 0, i)),
        out_shape=jax.ShapeDtypeStruct((b, B_W, l), BF16),
        scratch_shapes=[pltpu.VMEM((B_HEADS, nb, MOBA_BLOCK), F32),
                        pltpu.VMEM((2, MOBA_BLOCK, 2 * MOBA_BLOCK), F32),
                        pltpu.VMEM((B_HEADS // 2, 1, 2 * MOBA_BLOCK), F32),
                        pltpu.VMEM((B_HEADS // 2, 1, 2 * MOBA_BLOCK), F32),
                        pltpu.VMEM((B_HEADS // 2, SLAB, 2 * MOBA_BLOCK), F32)],
        compiler_params=_params(("parallel", "parallel", "arbitrary")),
        name="moba_attn",
    )(qt, q32t, qkv16, vt, kmean)


INT_MIN = -2 ** 31
NEG_BITS = int(np.float32(NEG).view(np.int32))


def _order_key(score):
    bits = pltpu.bitcast(jnp.where(score == 0.0, 0.0, score), jnp.int32)
    return bits ^ ((bits >> 31) & 0x7FFFFFFF)


SUBLANES = 8


def _fold(x, axis):
    if axis == 1:
        part = x[:, 0:LANES]
        for s in range(1, x.shape[1] // LANES):
            part = part + x[:, s * LANES:(s + 1) * LANES]
        return part
    return _rows_fold(x, jnp.sum)


def _count(key_ref, nch, preds, axis):
    shape = key_ref.shape[1:]
    part = (shape[0], LANES) if axis == 1 else (SUBLANES, shape[1])

    def body(c, accs):
        blk = key_ref[c]
        return tuple(a + _fold(jnp.where(p(blk), 1.0, 0.0), axis) for a, p in zip(accs, preds))

    accs = lax.fori_loop(0, nch, body, tuple(jnp.zeros(part, F32) for _ in preds))
    return tuple(jnp.sum(a, axis=axis, keepdims=True) for a in accs)


def _kth_largest_key(key_ref, nch, kk, axis):
    def cond(carry):
        it, _, _, n_ge = carry
        return jnp.logical_and(it < 32, jnp.max(n_ge - kk) > 0.5)

    def search(carry):
        it, t_u, bit, n_ge = carry
        cand_u = t_u | bit
        cand = cand_u ^ INT_MIN
        cnt, = _count(key_ref, nch, (lambda blk: blk >= cand,), axis)
        keep = cnt >= kk
        return it + 1, jnp.where(keep, cand_u, t_u), lax.shift_right_logical(bit, 1), jnp.where(keep, cnt, n_ge)

    n_all, = _count(key_ref, nch, (lambda blk: blk > INT_MIN,), axis)
    _, t_u, _, n_ge = lax.while_loop(
        cond, search, (jnp.int32(0), jnp.zeros(kk.shape, jnp.int32), jnp.full(kk.shape, INT_MIN, jnp.int32), n_all))
    return jnp.maximum(t_u ^ INT_MIN, INT_MIN + 1), n_ge


def _topk_bias(key_ref, out_ref, tri_ref, nch, kk, axis):
    tc = key_ref.shape[1 + axis]
    t, n_ge = _kth_largest_key(key_ref, nch, kk, axis)
    ties = jnp.max(n_ge - kk) > 0.5

    def put(c, take):
        if out_ref.dtype == jnp.int32:
            out_ref[c] = jnp.where(take, jnp.int32(0), jnp.int32(NEG_BITS))
        else:
            out_ref[c] = jnp.where(take, 0.0, NEG)

    @pl.when(jnp.logical_not(ties))
    def _():
        def body(c, _):
            put(c, key_ref[c] >= t)
            return 0
        lax.fori_loop(0, nch, body, 0)

    @pl.when(ties)
    def _():
        n_gt, = _count(key_ref, nch, (lambda blk: blk > t,), axis)
        need = kk - n_gt
        r = lax.broadcasted_iota(jnp.int32, (tc, tc), 0)
        cidx = lax.broadcasted_iota(jnp.int32, (tc, tc), 1)
        tri_ref[...] = jnp.where((r <= cidx) if axis == 1 else (cidx <= r), 1.0, 0.0).astype(BF16)

        def body(c, seen):
            blk = key_ref[c]
            eqm = blk == t
            eq16 = jnp.where(eqm, 1.0, 0.0).astype(BF16)
            rank = seen + (_dot(eq16, tri_ref[...]) if axis == 1 else _dot(tri_ref[...], eq16))
            put(c, jnp.logical_or(blk > t, jnp.logical_and(eqm, rank <= need)))
            return seen + jnp.sum(jnp.where(eqm, 1.0, 0.0), axis=axis, keepdims=True)
        lax.fori_loop(0, nch, body, jnp.zeros(kk.shape, F32))


DSA_SPLIT = 1


def _dsa_kernel(qt_ref, qit_ref, wit_ref, k_ref, vt_ref, ki_ref, o_ref,
                key_ref, tri_ref, s_ref, m_ref, l_ref, acc_ref, *, tq, tc, topk):
    i = pl.program_id(1)
    nch = ((i + 1) * tq - 1) // tc + 1
    qpos = i * tq + lax.broadcasted_iota(jnp.int32, (1, tq), 1)
    kk = jnp.minimum(topk, qpos + 1).astype(F32)

    def score_chunk(c, _):
        r0 = pl.multiple_of(c * tc, tc)
        kic = ki_ref[0, pl.ds(r0, tc), :]
        score = jnp.zeros((tc, tq), F32)
        for h in range(IDX_HEADS):
            lg = _dot(kic, qit_ref[0, h * SLAB:(h + 1) * SLAB, :])
            score = score + wit_ref[0, h:h + 1, :] * jnp.maximum(lg, 0.0)
        kpos = r0 + lax.broadcasted_iota(jnp.int32, (tc, tq), 0)
        key_ref[c] = jnp.where(kpos <= qpos, _order_key(score), INT_MIN)
        return 0

    lax.fori_loop(0, nch, score_chunk, 0)
    _topk_bias(key_ref, key_ref, tri_ref, nch, kk, axis=0)
    _flash_init(m_ref, l_ref, acc_ref)
    group = C_HEADS // C_KV_HEADS

    def scores(c, g):
        r0 = pl.multiple_of(c * tc, tc)
        qg = jnp.concatenate([qt_ref[0, h * SLAB:(h + 1) * SLAB, :] for h in range(g * group, (g + 1) * group)], axis=1)
        bias = jnp.concatenate([pltpu.bitcast(key_ref[c], F32)] * group, axis=1)
        s_ref[g % 2] = _dot(k_ref[0, pl.ds(r0, tc), (g // 2) * SLAB:(g // 2 + 1) * SLAB], qg) + bias

    def attend(c, _):
        for g in range(C_KV_HEADS):
            if g + 1 < C_KV_HEADS:
                scores(c, g + 1)
            else:
                scores(jnp.minimum(c + 1, nch - 1), 0)
            _flash_update_t([s_ref[g % 2]], [vt_ref[0, c, g * HEAD_DIM:(g + 1) * HEAD_DIM, :]], m_ref, l_ref, acc_ref, g)
        return 0

    scores(0, 0)
    lax.fori_loop(0, nch, attend, 0)
    for h in range(C_HEADS):
        g, hh = h // group, h % group
        cols = slice(hh * tq, (hh + 1) * tq)
        o_ref[0, h * HEAD_DIM:(h + 1) * HEAD_DIM, :] = (acc_ref[g, :, cols] / l_ref[g, :, cols]).astype(BF16)


def _dsa_attn(qt, qit, wit, qkv16, vt4, tq, tc):
    b, l, _ = qkv16.shape
    topk = min(DSA_TOPK, l // 4)
    nc = l // tc
    gq = (C_HEADS // C_KV_HEADS) * tq
    return pl.pallas_call(
        functools.partial(_dsa_kernel, tq=tq, tc=tc, topk=topk),
        grid=(b, l // tq),
        in_specs=[
            pl.BlockSpec((1, C_HEADS * SLAB, tq), lambda bi, i: (bi, 0, i)),
            pl.BlockSpec((1, IDX_HEADS * SLAB, tq), lambda bi, i: (bi, 0, i)),
            pl.BlockSpec((1, IDX_HEADS, tq), lambda bi, i: (bi, 0, i)),
            pl.BlockSpec((1, l, C_KV), lambda bi, i: (bi, 0, C_Q // C_KV)),
            pl.BlockSpec((1, nc, C_KV, tc), lambda bi, i: (bi, 0, 0, 0)),
            pl.BlockSpec((1, l, SLAB), lambda bi, i: (bi, 0, ODD_KI_COL // SLAB)),
        ],
        out_specs=pl.BlockSpec((1, C_Q, tq), lambda bi, i: (bi, 0, i)),
        out_shape=jax.ShapeDtypeStruct((b, C_Q, l), BF16),
        scratch_shapes=[pltpu.VMEM((nc, tc, tq), jnp.int32), pltpu.VMEM((tc, tc), BF16),
                        pltpu.VMEM((2, tc, gq), F32),
                        pltpu.VMEM((C_KV_HEADS, 1, gq), F32), pltpu.VMEM((C_KV_HEADS, 1, gq), F32),
                        pltpu.VMEM((C_KV_HEADS, HEAD_DIM, gq), F32)],
        compiler_params=_params(("parallel", "arbitrary")),
        name="dsa_attn",
    )(qt, qit, wit, qkv16, vt4, qkv16)


NEW_PAD = 16
EVEN_BLOCKS = 2
EVEN_PAGES = EVEN_BLOCKS * MOBA_BLOCK // PAGE_SIZE


def _fold_heads(x, n_groups, n_new):
    return jnp.sum(x.reshape(n_groups, n_new, x.shape[1]), axis=0)


def _dec_even_kernel(pt_ref, lam_ref, qbd_ref, *refs, n_new, n_blocks, out_scale):
    kpages = refs[:EVEN_PAGES]
    vpages = refs[EVEN_PAGES:2 * EVEN_PAGES]
    knew_ref, vnew_ref, g_ref, o_ref, ma_ref, la_ref, acca_ref, mb_ref, lb_ref, accb_ref, km_ref = refs[2 * EVEN_PAGES:]
    p = pl.program_id(1)
    ra = 8 * n_new
    qbd = qbd_ref[0]

    @pl.when(p == 0)
    def _():
        ma_ref[...] = jnp.full_like(ma_ref, NEG)
        la_ref[...] = jnp.zeros_like(la_ref)
        acca_ref[...] = jnp.zeros_like(acca_ref)
        km_ref[...] = jnp.zeros_like(km_ref)

    def attend(k16, v16, mask, blk):
        s = _dot_nt(qbd, k16)
        if mask is not None:
            s = jnp.where(mask, s, NEG)
        sa, sb = s[:ra], s[ra:]
        m_prev = ma_ref[...]
        m_new = jnp.maximum(m_prev, jnp.max(sa, axis=1, keepdims=True))
        alpha = jnp.exp2(m_prev - m_new)
        pa = jnp.exp2(sa - m_new)
        la_ref[...] = alpha * la_ref[...] + jnp.sum(pa, axis=1, keepdims=True)
        acca_ref[...] = alpha * acca_ref[...] + _dot(pa.astype(BF16), v16[:, :A_V])
        ma_ref[...] = m_new
        mb = jnp.max(sb, axis=1, keepdims=True)
        pb = jnp.exp2(sb - mb)
        mb_ref[blk] = mb
        lb_ref[blk] = jnp.sum(pb, axis=1, keepdims=True)
        accb_ref[blk] = _dot(pb.astype(BF16), v16[:, A_V:])

    per_block = MOBA_BLOCK // PAGE_SIZE
    n_steps = n_blocks // EVEN_BLOCKS

    @pl.when(p < n_steps)
    def _():
        for sb in range(EVEN_BLOCKS):
            pages = slice(sb * per_block, (sb + 1) * per_block)
            k32 = jnp.concatenate([r[...] for r in kpages[pages]], axis=0)
            v32 = jnp.concatenate([r[...] for r in vpages[pages]], axis=0)
            blk = p * EVEN_BLOCKS + sb
            km_ref[pl.ds(blk, 1), :] = jnp.mean(k32[:, A_QK:], axis=0, keepdims=True)
            attend(k32.astype(BF16), v32.astype(BF16), None, blk)

    @pl.when(p == n_steps)
    def _():
        rows = 16 * n_new
        t = lax.broadcasted_iota(jnp.int32, (rows, NEW_PAD), 0) % n_new
        j = lax.broadcasted_iota(jnp.int32, (rows, NEW_PAD), 1)
        attend(knew_ref[0], vnew_ref[0], j <= t, n_blocks)

        lam = lam_ref[0]
        r = acca_ref[...] / la_ref[...]
        m_idx = lax.broadcasted_iota(jnp.int32, r.shape, 0) // n_new
        col = lax.broadcasted_iota(jnp.int32, r.shape, 1)
        coef = jnp.where(m_idx % 2 == 0, 1.0, -lam)
        oa = _fold_heads(jnp.where(col // A_VDIM == m_idx // 2, r * coef, 0.0), 8, n_new)
        for h in range(A_HEADS):
            cs = slice(h * A_VDIM, (h + 1) * A_VDIM)
            o_ref[0, :, cs] = (_rms(oa[:, cs], g_ref[...]) * out_scale).astype(BF16)

        nbp = km_ref.shape[0]
        gate = _dot_nt_f32(qbd[ra:, A_QK:].astype(F32), km_ref[...])
        blk = lax.broadcasted_iota(jnp.int32, gate.shape, 1)
        sel = _top3_select(jnp.where(blk < n_blocks, gate, -jnp.inf), n_blocks)
        chosen = [sel[:, n:n + 1] > 0.5 for n in range(n_blocks)]
        m_all = mb_ref[n_blocks]
        for n in range(n_blocks):
            m_all = jnp.where(chosen[n], jnp.maximum(m_all, mb_ref[n]), m_all)
        w_own = jnp.exp2(mb_ref[n_blocks] - m_all)
        l_all, acc = w_own * lb_ref[n_blocks], w_own * accb_ref[n_blocks]
        for n in range(n_blocks):
            w = jnp.where(chosen[n], jnp.exp2(mb_ref[n] - m_all), 0.0)
            l_all = l_all + w * lb_ref[n]
            acc = acc + w * accb_ref[n]
        r = acc / l_all
        h_idx = lax.broadcasted_iota(jnp.int32, r.shape, 0) // n_new
        col = lax.broadcasted_iota(jnp.int32, r.shape, 1)
        ob = _fold_heads(jnp.where(col // HEAD_DIM == h_idx, r, 0.0), 8, n_new)
        o_ref[0, :, A_V:] = ob.astype(BF16)


def _block_diag_rows(q, n_maps, width):
    s, t, c = q.shape
    keep = (jnp.arange(c)[None, :] // width) == jnp.arange(n_maps)[:, None]
    return jnp.where(keep[None, :, None, :], q[:, None, :, :], 0).reshape(s, n_maps * t, c).astype(q.dtype)


def _dec_even(page_table, lam, qbd, cache_k, cache_v, layer, knew, vnew, subln_g, out_scale):
    nseq, n_pages = page_table.shape
    n_new = qbd.shape[1] // 16
    n_blocks = n_pages * PAGE_SIZE // MOBA_BLOCK
    n_steps = n_pages // EVEN_PAGES
    width = cache_k.shape[-1]

    def page_spec(r):
        return pl.BlockSpec((None, None, PAGE_SIZE, width),
                            lambda s, p, pt, lm: (pt[s, jnp.minimum(p, n_steps - 1) * EVEN_PAGES + r], layer, 0, 0))

    seq_spec = lambda shape: pl.BlockSpec((1,) + shape, lambda s, p, pt, lm: (s, 0, 0))
    ra = 8 * n_new
    nbp = -(-(n_blocks + 1) // 8) * 8
    return pl.pallas_call(
        functools.partial(_dec_even_kernel, n_new=n_new, n_blocks=n_blocks, out_scale=out_scale),
        grid_spec=pltpu.PrefetchScalarGridSpec(
            num_scalar_prefetch=2,
            grid=(nseq, n_steps + 1),
            in_specs=[seq_spec((16 * n_new, width))]
            + [page_spec(r) for r in range(EVEN_PAGES)] + [page_spec(r) for r in range(EVEN_PAGES)]
            + [seq_spec((NEW_PAD, width)), seq_spec((NEW_PAD, width)),
               pl.BlockSpec((1, A_VDIM), lambda s, p, pt, lm: (0, 0))],
            out_specs=seq_spec((n_new, A_V + B_W)),
            scratch_shapes=[pltpu.VMEM((ra, 1), F32), pltpu.VMEM((ra, 1), F32), pltpu.VMEM((ra, A_V), F32),
                            pltpu.VMEM((n_blocks + 1, ra, 1), F32), pltpu.VMEM((n_blocks + 1, ra, 1), F32),
                            pltpu.VMEM((n_blocks + 1, ra, B_W), F32), pltpu.VMEM((nbp, B_W), F32)],
        ),
        out_shape=jax.ShapeDtypeStruct((nseq, n_new, A_V + B_W), BF16),
        compiler_params=_params(("parallel", "arbitrary")),
        name="dec_even",
    )(page_table, lam, qbd, *([cache_k] * EVEN_PAGES), *([cache_v] * EVEN_PAGES), knew, vnew, subln_g)


ODD_PAGES = 8
ODD_CHUNK = ODD_PAGES * PAGE_SIZE


def _dec_select_kernel(pt_ref, qi_ref, wi_ref, *refs, n_new, n_chunks, topk):
    pages = refs[:ODD_PAGES]
    kinew_ref, bias_ref, key_ref, tri_ref = refs[ODD_PAGES:]
    p = pl.program_id(1)
    qi = qi_ref[0]
    wi = wi_ref[0]

    def scores(ki16):
        lg = _dot_nt(qi, ki16)
        return _fold_heads(wi * jnp.maximum(lg, 0.0), IDX_HEADS, n_new)

    @pl.when(p < n_chunks)
    def _():
        for r in range(ODD_PAGES):
            key_ref[p, :, r * PAGE_SIZE:(r + 1) * PAGE_SIZE] = _order_key(scores(pages[r][...].astype(BF16)))

    @pl.when(p == n_chunks)
    def _():
        sc = scores(kinew_ref[0])
        t = lax.broadcasted_iota(jnp.int32, sc.shape, 0)
        j = lax.broadcasted_iota(jnp.int32, sc.shape, 1)
        key_ref[n_chunks] = jnp.full((n_new, ODD_CHUNK), INT_MIN, jnp.int32)
        key_ref[n_chunks, :, 0:PAGE_SIZE] = jnp.where(j <= t, _order_key(sc), INT_MIN)
        qpos = n_chunks * ODD_CHUNK + lax.broadcasted_iota(jnp.int32, (n_new, 1), 0)
        kk = jnp.minimum(topk, qpos + 1).astype(F32)
        _topk_bias(key_ref, bias_ref.at[0], tri_ref, n_chunks + 1, kk, axis=1)


def _dec_select(page_table, qi, wi, cache_ki, layer, kinew):
    nseq, n_pages = page_table.shape
    n_new = qi.shape[1] // IDX_HEADS
    n_chunks = n_pages // ODD_PAGES
    topk = min(DSA_TOPK, (n_pages * PAGE_SIZE + n_new) // 4)

    def page_spec(r):
        return pl.BlockSpec((None, None, PAGE_SIZE, IDX_DIM),
                            lambda s, p, pt: (pt[s, jnp.minimum(p, n_chunks - 1) * ODD_PAGES + r], layer, 0, 0))

    return pl.pallas_call(
        functools.partial(_dec_select_kernel, n_new=n_new, n_chunks=n_chunks, topk=topk),
        grid_spec=pltpu.PrefetchScalarGridSpec(
            num_scalar_prefetch=1,
            grid=(nseq, n_chunks + 1),
            in_specs=[pl.BlockSpec((1, IDX_HEADS * n_new, IDX_DIM), lambda s, p, pt: (s, 0, 0)),
                      pl.BlockSpec((1, IDX_HEADS * n_new, 1), lambda s, p, pt: (s, 0, 0))]
            + [page_spec(r) for r in range(ODD_PAGES)]
            + [pl.BlockSpec((1, PAGE_SIZE, IDX_DIM), lambda s, p, pt: (s, 0, 0))],
            out_specs=pl.BlockSpec((1, n_chunks + 1, n_new, ODD_CHUNK), lambda s, p, pt: (s, 0, 0, 0)),
            scratch_shapes=[pltpu.VMEM((n_chunks + 1, n_new, ODD_CHUNK), jnp.int32),
                            pltpu.VMEM((ODD_CHUNK, ODD_CHUNK), BF16)],
        ),
        out_shape=jax.ShapeDtypeStruct((nseq, n_chunks + 1, n_new, ODD_CHUNK), F32),
        compiler_params=_params(("parallel", "arbitrary")),
        name="dec_select",
    )(page_table, qi, wi, *([cache_ki] * ODD_PAGES), kinew)


def _dec_odd_kernel(pt_ref, qbd_ref, bias_ref, *refs, n_new, n_chunks):
    kpages = refs[:ODD_PAGES]
    vpages = refs[ODD_PAGES:2 * ODD_PAGES]
    knew_ref, vnew_ref, o_ref, m_ref, l_ref, acc_ref = refs[2 * ODD_PAGES:]
    p = pl.program_id(1)
    qbd = qbd_ref[0]

    @pl.when(p == 0)
    def _():
        m_ref[...] = jnp.full_like(m_ref, NEG)
        l_ref[...] = jnp.zeros_like(l_ref)
        acc_ref[...] = jnp.zeros_like(acc_ref)

    def attend(k16, v16, bias):
        s = _dot_nt(qbd, k16) + jnp.tile(bias, (C_HEADS, 1))
        m_prev = m_ref[...]
        m_new = jnp.maximum(m_prev, jnp.max(s, axis=1, keepdims=True))
        alpha = jnp.exp2(m_prev - m_new)
        pr = jnp.exp2(s - m_new)
        l_ref[...] = alpha * l_ref[...] + jnp.sum(pr, axis=1, keepdims=True)
        acc_ref[...] = alpha * acc_ref[...] + _dot(pr.astype(BF16), v16)
        m_ref[...] = m_new

    @pl.when(p < n_chunks)
    def _():
        k16 = jnp.concatenate([r[...] for r in kpages], axis=0).astype(BF16)
        v16 = jnp.concatenate([r[...] for r in vpages], axis=0).astype(BF16)
        attend(k16, v16, bias_ref[0, 0])

    @pl.when(p == n_chunks)
    def _():
        attend(knew_ref[0], vnew_ref[0], bias_ref[0, 0, :, 0:NEW_PAD])
        r = acc_ref[...] / l_ref[...]
        group = C_HEADS // C_KV_HEADS
        for h in range(C_HEADS):
            g = h // group
            o_ref[0, :, h * HEAD_DIM:(h + 1) * HEAD_DIM] = (
                r[h * n_new:(h + 1) * n_new, g * HEAD_DIM:(g + 1) * HEAD_DIM].astype(BF16))


def _dec_odd(page_table, qbd, bias, cache_k, cache_v, layer, knew, vnew):
    nseq, n_pages = page_table.shape
    n_new = qbd.shape[1] // C_HEADS
    n_chunks = n_pages // ODD_PAGES

    def page_spec(r):
        return pl.BlockSpec((None, None, PAGE_SIZE, C_KV),
                            lambda s, p, pt: (pt[s, jnp.minimum(p, n_chunks - 1) * ODD_PAGES + r], layer, 0, 0))

    seq_spec = lambda shape: pl.BlockSpec((1,) + shape, lambda s, p, pt: (s, 0, 0))
    rows = C_HEADS * n_new
    return pl.pallas_call(
        functools.partial(_dec_odd_kernel, n_new=n_new, n_chunks=n_chunks),
        grid_spec=pltpu.PrefetchScalarGridSpec(
            num_scalar_prefetch=1,
            grid=(nseq, n_chunks + 1),
            in_specs=[seq_spec((rows, C_KV)),
                      pl.BlockSpec((1, 1, n_new, ODD_CHUNK), lambda s, p, pt: (s, p, 0, 0))]
            + [page_spec(r) for r in range(ODD_PAGES)] + [page_spec(r) for r in range(ODD_PAGES)]
            + [seq_spec((NEW_PAD, C_KV)), seq_spec((NEW_PAD, C_KV))],
            out_specs=seq_spec((n_new, C_Q)),
            scratch_shapes=[pltpu.VMEM((rows, 1), F32), pltpu.VMEM((rows, 1), F32), pltpu.VMEM((rows, C_KV), F32)],
        ),
        out_shape=jax.ShapeDtypeStruct((nseq, n_new, C_Q), BF16),
        compiler_params=_params(("parallel", "arbitrary")),
        name="dec_odd",
    )(page_table, qbd, bias, *([cache_k] * ODD_PAGES), *([cache_v] * ODD_PAGES), knew, vnew)


ROW_TILE = 1024
FFN_TF = 256


def _pad_rows(a, rows):
    return jnp.pad(a, ((0, 0), (0, rows - a.shape[1]), (0, 0)))


def _gqa_rows(q):
    s, t, nh, d = q.shape
    group = nh // C_KV_HEADS
    keep = (jnp.arange(nh)[:, None] // group) == jnp.arange(C_KV_HEADS)[None, :]
    qh = jnp.swapaxes(q, 1, 2)
    out = jnp.where(keep[None, :, None, :, None], qh[:, :, :, None, :], 0)
    return out.reshape(s, nh * t, C_KV_HEADS * d).astype(q.dtype)


def kernel(x_prompt, x_sample, cache_k_even, cache_v_even, cache_k_odd, cache_v_odd, cache_kidx_odd, state_conv,
           page_table, g_mix, g_ffn, g_final, w_in_even, w_out_even, lam_even, subln_even, w_in_odd, w_out_odd,
           w_gate, w_up, conv_w, conv_b, w_down):
    bsz, seq, d = x_prompt.shape
    nseq, n_new, _ = x_sample.shape
    depth = g_mix.shape[0]
    ff = w_gate.shape[-1]
    past_len = page_table.shape[1] * PAGE_SIZE
    bf = lambda a: a.astype(BF16)

    cos_p, sin_p = _rope_tables(jnp.arange(seq, dtype=jnp.int32))
    cos_s, sin_s = _rope_tables(jnp.tile(past_len + jnp.arange(n_new, dtype=jnp.int32), nseq))
    ecols, ocols = _even_cols(), _odd_cols()
    qb0 = 2 * A_QK + A_V
    tm = min(ROW_TILE, seq)
    ms = nseq * n_new

    hp = x_prompt.reshape(bsz * seq, d)
    hs = x_sample.reshape(ms, d)
    ke_p, ve_p, ko_p, vo_p, kio_p, cs_p = [], [], [], [], [], []
    ke_s, ve_s, ko_s, vo_s, kio_s, cs_s = [], [], [], [], [], []
    for layer in range(depth):
        e = layer // 2
        g = g_mix[layer][None]
        if layer % 2 == 0:
            w16, wo = bf(w_in_even[e]), bf(w_out_even[e])
            lam_init = 0.8 - 0.6 * math.exp(-0.3 * layer)
            lp = lam_even[e].astype(F32)
            lam = (jnp.exp(jnp.sum(lp[0] * lp[1])) - jnp.exp(jnp.sum(lp[2] * lp[3])) + lam_init).reshape(1)
            sg = subln_even[e][None]
            out_scale = 1.0 - lam_init

            p32, p16 = _inproj(hp, g, w16, cos_p, sin_p, *ecols, tm=tm)
            p32, p16 = p32.reshape(bsz, seq, -1), p16.reshape(bsz, seq, -1)
            pair = jnp.arange(2 * A_HEADS) % 2
            qat = _heads_t(p16[..., :A_QK].reshape(bsz, seq, 2 * A_HEADS, HEAD_DIM), pair)
            vat = jnp.swapaxes(p16[..., 2 * A_QK:qb0], 1, 2)
            oat = _diff_attn(lam, qat, p16, vat, sg.reshape(A_VDIM, 1), out_scale, tq=256, tk=512)
            kmean = _blockmean(p32, (qb0 + B_W) // B_W, B_W, nb_step=min(8, seq // MOBA_BLOCK))
            qbt = _heads_t(p16[..., qb0:qb0 + B_W].reshape(bsz, seq, B_HEADS, HEAD_DIM), pair)
            q32t = jnp.swapaxes(p32[..., qb0:qb0 + B_W], 1, 2)
            vbt = jnp.swapaxes(p16[..., qb0 + 2 * B_W:], 1, 2)
            obt = _moba_attn(qbt, q32t, p16, vbt, kmean, sub=2)
            o = jnp.swapaxes(jnp.concatenate([oat, obt], axis=1), 1, 2)
            hp = _outproj([o.reshape(bsz * seq, -1)], [wo], hp, tm=tm)
            ke_p.append(jnp.concatenate([p32[..., A_QK:2 * A_QK], p32[..., qb0 + B_W:qb0 + 2 * B_W]], -1))
            ve_p.append(jnp.concatenate([p32[..., 2 * A_QK:qb0], p32[..., qb0 + 2 * B_W:]], -1))

            s32, s16 = _inproj(hs, g, w16, cos_s, sin_s, *ecols, tm=ms)
            s32, s16 = s32.reshape(nseq, n_new, -1), s16.reshape(nseq, n_new, -1)
            qbd = _block_diag_rows(jnp.concatenate([s16[..., :A_QK], s16[..., qb0:qb0 + B_W]], -1), 16, HEAD_DIM)
            knew = jnp.concatenate([s16[..., A_QK:2 * A_QK], s16[..., qb0 + B_W:qb0 + 2 * B_W]], -1)
            vnew = jnp.concatenate([s16[..., 2 * A_QK:qb0], s16[..., qb0 + 2 * B_W:]], -1)
            o = _dec_even(page_table, lam, qbd, cache_k_even, cache_v_even, e,
                          _pad_rows(knew, NEW_PAD), _pad_rows(vnew, NEW_PAD), sg, out_scale)
            hs = _outproj([o.reshape(ms, -1)], [wo], hs, tm=ms)
            ke_s.append(jnp.concatenate([s32[..., A_QK:2 * A_QK], s32[..., qb0 + B_W:qb0 + 2 * B_W]], -1))
            ve_s.append(jnp.concatenate([s32[..., 2 * A_QK:qb0], s32[..., qb0 + 2 * B_W:]], -1))
        else:
            w16 = bf(jnp.pad(w_in_odd[e], ((0, 0), (0, ODD_NPAD - ODD_N))))
            wo = bf(w_out_odd[e])
            kc, vc, kic, wic = C_Q, C_Q + C_KV, ODD_KI_COL, ODD_KI_COL + IDX_DIM

            p32, p16 = _inproj(hp, g, w16, cos_p, sin_p, *ocols, tm=tm)
            p32, p16 = p32.reshape(bsz, seq, -1), p16.reshape(bsz, seq, -1)
            tc = min(512, seq)
            qt = _heads_t(p16[..., :C_Q].reshape(bsz, seq, C_HEADS, HEAD_DIM),
                          (jnp.arange(C_HEADS) // (C_HEADS // C_KV_HEADS)) % 2)
            qit = _heads_t(p16[..., vc + C_KV:kic].reshape(bsz, seq, IDX_HEADS, IDX_DIM), jnp.zeros(IDX_HEADS, jnp.int32))
            wit = jnp.swapaxes(p32[..., wic:wic + IDX_HEADS], 1, 2)
            vt4 = jnp.swapaxes(p16[..., vc:vc + C_KV].reshape(bsz, seq // tc, tc, C_KV), 2, 3)
            ot = _dsa_attn(qt, qit, wit, p16, vt4, tq=256, tc=tc)
            hp = _outproj([jnp.swapaxes(ot, 1, 2).reshape(bsz * seq, -1)], [wo], hp, tm=tm)
            ko_p.append(p32[..., kc:kc + C_KV])
            vo_p.append(p32[..., vc:vc + C_KV])
            kio_p.append(p32[..., kic:kic + IDX_DIM])

            s32, s16 = _inproj(hs, g, w16, cos_s, sin_s, *ocols, tm=ms)
            s32, s16 = s32.reshape(nseq, n_new, -1), s16.reshape(nseq, n_new, -1)
            qi = s16[..., vc + C_KV:kic].reshape(nseq, n_new, IDX_HEADS, IDX_DIM)
            qi = jnp.swapaxes(qi, 1, 2).reshape(nseq, IDX_HEADS * n_new, IDX_DIM)
            wi = jnp.swapaxes(s32[..., wic:wic + IDX_HEADS], 1, 2).reshape(nseq, IDX_HEADS * n_new, 1)
            bias = _dec_select(page_table, qi, wi, cache_kidx_odd, e, _pad_rows(s16[..., kic:kic + IDX_DIM], PAGE_SIZE))
            qbd = _gqa_rows(s16[..., :C_Q].reshape(nseq, n_new, C_HEADS, HEAD_DIM))
            o = _dec_odd(page_table, qbd, bias, cache_k_odd, cache_v_odd, e,
                         _pad_rows(s16[..., kc:kc + C_KV], NEW_PAD), _pad_rows(s16[..., vc:vc + C_KV], NEW_PAD))
            hs = _outproj([o.reshape(ms, -1)], [wo], hs, tm=ms)
            ko_s.append(s32[..., kc:kc + C_KV])
            vo_s.append(s32[..., vc:vc + C_KV])
            kio_s.append(s32[..., kic:kic + IDX_DIM])

        last = layer == depth - 1
        ffn_w = (g_ffn[layer][None], bf(w_gate[layer]), bf(w_up[layer]), conv_w[layer], conv_b[layer][None],
                 bf(w_down[layer]), g_final[None])
        hp, tail = _ffn(hp, *ffn_w, seq_len=seq, tm=tm, tf=FFN_TF, final_norm=last)
        cs_p.append(tail.reshape(bsz, seq // tm, 8, ff)[:, -1, 8 - (CONV_W - 1):])
        hs, gout = _ffn_s(hs, state_conv[layer], *ffn_w, seq_len=n_new, tf=FFN_TF, final_norm=last)
        cs_s.append(gout.reshape(nseq, n_new, ff)[:, n_new - (CONV_W - 1):])

    def to_pages(rows):
        r = jnp.stack(rows, 1)
        b, nl, s, w = r.shape
        return r.reshape(b, nl, s // PAGE_SIZE, PAGE_SIZE, w).transpose(0, 2, 1, 3, 4)

    return (hp.reshape(bsz, seq, d), hs.reshape(nseq, n_new, d),
            to_pages(ke_p), to_pages(ve_p), to_pages(ko_p), to_pages(vo_p), to_pages(kio_p), jnp.stack(cs_p, 0),
            jnp.stack(ke_s, 1), jnp.stack(ve_s, 1), jnp.stack(ko_s, 1), jnp.stack(vo_s, 1), jnp.stack(kio_s, 1),
            jnp.stack(cs_s, 0))
```

```python
import functools
import math

import jax
import jax.numpy as jnp
import numpy as np
from jax import lax
from jax.experimental import pallas as pl
from jax.experimental.pallas import tpu as pltpu

F32 = jnp.float32
BF16 = jnp.bfloat16

HEAD_DIM = 64
A_HEADS = 4
A_VDIM = 2 * HEAD_DIM
B_HEADS = 8
MOBA_BLOCK = 256
MOBA_TOPK = 3
C_HEADS = 16
C_KV_HEADS = 4
IDX_HEADS = 8
IDX_DIM = 64
DSA_TOPK = 256
CONV_W = 3
ROPE_THETA = 10000.0
EPS = 1e-6
PAGE_SIZE = 128
A_QK = A_HEADS * 2 * HEAD_DIM
A_V = A_HEADS * A_VDIM
B_W = B_HEADS * HEAD_DIM
C_Q = C_HEADS * HEAD_DIM
C_KV = C_KV_HEADS * HEAD_DIM
IDX_Q = IDX_HEADS * IDX_DIM
IDX_W_SCALE = IDX_Q ** -0.5
ATTN_SCALE = HEAD_DIM ** -0.5
Q_SCALE = ATTN_SCALE * math.log2(math.e)

LANES = 128
NEG = -1e30
VMEM_LIMIT = 56 * 1024 * 1024

EVEN_N = 2 * A_QK + A_V + 3 * B_W
ODD_N = C_Q + 2 * C_KV + IDX_Q + IDX_DIM + IDX_HEADS
ODD_NPAD = 2304
ODD_KI_COL = C_Q + 2 * C_KV + IDX_Q
PROJ_TN = 768


def _params(sem, vmem=VMEM_LIMIT):
    return pltpu.CompilerParams(dimension_semantics=sem, vmem_limit_bytes=vmem)


def _dot_nt(a, b):
    return lax.dot_general(a, b, (((1,), (1,)), ((), ())), preferred_element_type=F32)


def _dot(a, b):
    return jnp.dot(a, b, preferred_element_type=F32)


def _rms(x, g):
    return x * lax.rsqrt(jnp.mean(x * x, axis=-1, keepdims=True) + EPS) * g


def _inproj_kernel(x_ref, g_ref, w_ref, cos_ref, sin_ref, rmask_ref, s32_ref, s16_ref,
                   o32_ref, o16_ref, xn_ref):
    @pl.when(pl.program_id(1) == 0)
    def _():
        xn_ref[...] = _rms(x_ref[...], g_ref[...]).astype(BF16)

    y = _dot(xn_ref[...], w_ref[...])
    tn = y.shape[1]
    reps = tn // LANES
    cos = jnp.tile(cos_ref[...], (1, reps))
    sin = jnp.tile(sin_ref[...], (1, reps))
    lane = lax.broadcasted_iota(jnp.int32, y.shape, 1)
    first_half = (lane % HEAD_DIM) < (HEAD_DIM // 2)
    partner = jnp.where(first_half, pltpu.roll(y, tn - HEAD_DIM // 2, 1), pltpu.roll(y, HEAD_DIM // 2, 1))
    roped = y * cos + partner * sin
    out = jnp.where(rmask_ref[...] > 0.5, roped, y) * s32_ref[...]
    o32_ref[...] = out
    o16_ref[...] = (out * s16_ref[...]).astype(BF16)


def _inproj(x, g, w16, cos, sin, rmask, s32, s16, tm):
    m, d = x.shape
    n = w16.shape[1]
    tn = PROJ_TN
    tblocks = cos.shape[0] // tm
    return pl.pallas_call(
        _inproj_kernel,
        grid=(m // tm, n // tn),
        in_specs=[
            pl.BlockSpec((tm, d), lambda i, j: (i, 0)),
            pl.BlockSpec((1, d), lambda i, j: (0, 0)),
            pl.BlockSpec((d, tn), lambda i, j: (0, j)),
            pl.BlockSpec((tm, LANES), lambda i, j: (i % tblocks, 0)),
            pl.BlockSpec((tm, LANES), lambda i, j: (i % tblocks, 0)),
            pl.BlockSpec((1, tn), lambda i, j: (0, j)),
            pl.BlockSpec((1, tn), lambda i, j: (0, j)),
            pl.BlockSpec((1, tn), lambda i, j: (0, j)),
        ],
        out_specs=[
            pl.BlockSpec((tm, tn), lambda i, j: (i, j)),
            pl.BlockSpec((tm, tn), lambda i, j: (i, j)),
        ],
        out_shape=[jax.ShapeDtypeStruct((m, n), F32), jax.ShapeDtypeStruct((m, n), BF16)],
        scratch_shapes=[pltpu.VMEM((tm, d), BF16)],
        compiler_params=_params(("parallel", "arbitrary")),
        name="inproj",
    )(x, g, w16, cos, sin, rmask, s32, s16)


def _rope_tables(pos):
    half = HEAD_DIM // 2
    inv = ROPE_THETA ** (-jnp.arange(half, dtype=F32) * 2.0 / HEAD_DIM)
    ang = pos.astype(F32)[:, None] * inv[None, :]
    cos = jnp.tile(jnp.cos(ang), (1, LANES // half))
    sin = jnp.tile(jnp.concatenate([-jnp.sin(ang), jnp.sin(ang)], axis=1), (1, LANES // HEAD_DIM))
    return cos, sin


def _col_rows(n, rope_ranges, scale32, scale16):
    cols = jnp.arange(n)
    rmask = jnp.zeros((n,), F32)
    for lo, hi in rope_ranges:
        rmask = jnp.where((cols >= lo) & (cols < hi), 1.0, rmask)
    s32 = jnp.ones((n,), F32)
    for lo, hi, v in scale32:
        s32 = jnp.where((cols >= lo) & (cols < hi), v, s32)
    s16 = jnp.ones((n,), F32)
    for lo, hi, v in scale16:
        s16 = jnp.where((cols >= lo) & (cols < hi), v, s16)
    return rmask[None], s32[None], s16[None]


def _even_cols():
    o = A_QK + A_QK + A_V
    return _col_rows(EVEN_N, [(0, 2 * A_QK), (o, o + 2 * B_W)], [],
                     [(0, A_QK, Q_SCALE), (o, o + B_W, Q_SCALE)])


def _odd_cols():
    return _col_rows(ODD_NPAD, [(0, C_Q + C_KV), (C_Q + 2 * C_KV, ODD_KI_COL + IDX_DIM)],
                     [(ODD_KI_COL + IDX_DIM, ODD_N, IDX_W_SCALE)], [(0, C_Q, Q_SCALE)])


def _outproj_kernel(*refs, n_in):
    xs, ws, h_ref, o_ref = refs[:n_in], refs[n_in:2 * n_in], refs[2 * n_in], refs[2 * n_in + 1]
    acc = h_ref[...]
    for x_ref, w_ref in zip(xs, ws):
        acc = acc + _dot(x_ref[...], w_ref[...])
    o_ref[...] = acc


def _outproj(xs, ws, h, tm):
    m, d = h.shape
    n_in = len(xs)
    in_specs = [pl.BlockSpec((tm, x.shape[1]), lambda i: (i, 0)) for x in xs]
    in_specs += [pl.BlockSpec(w.shape, lambda i: (0, 0)) for w in ws]
    in_specs += [pl.BlockSpec((tm, d), lambda i: (i, 0))]
    return pl.pallas_call(
        functools.partial(_outproj_kernel, n_in=n_in),
        grid=(m // tm,),
        in_specs=in_specs,
        out_specs=pl.BlockSpec((tm, d), lambda i: (i, 0)),
        out_shape=jax.ShapeDtypeStruct((m, d), F32),
        compiler_params=_params(("parallel",)),
        name="outproj",
    )(*xs, *ws, h)


FFN_HALO = 16


def _ffn_kernel(h_ref, halo_ref, g_ref, wg_ref, wu_ref, cw_ref, cb_ref, wd_ref, gf_ref,
                o_ref, tail_ref, xn_ref, gext_ref, acc_ref, *, tiles_per_seq, final_norm):
    i, f = pl.program_id(0), pl.program_id(1)
    tm = h_ref.shape[0]

    @pl.when(f == 0)
    def _():
        xn_ref[0:FFN_HALO, :] = _rms(halo_ref[...], g_ref[...]).astype(BF16)
        xn_ref[FFN_HALO:, :] = _rms(h_ref[...], g_ref[...]).astype(BF16)
        acc_ref[...] = jnp.zeros_like(acc_ref)

    xn = xn_ref[...]
    gext = _dot(xn, wg_ref[...])
    first = (i % tiles_per_seq) == 0
    row = lax.broadcasted_iota(jnp.int32, gext.shape, 0)
    gext_ref[...] = jnp.where(jnp.logical_and(first, row < FFN_HALO), 0.0, gext)
    g = gext_ref[FFN_HALO:, :]
    p1 = gext_ref[pl.ds(FFN_HALO - 1, tm), :]
    p2 = gext_ref[pl.ds(FFN_HALO - 2, tm), :]
    cw = cw_ref[...]
    c = cb_ref[...] + p2 * cw[0:1] + p1 * cw[1:2] + g * cw[2:3]
    u = _dot(xn[FFN_HALO:], wu_ref[...])
    a = (c * jax.nn.sigmoid(c) * u).astype(BF16)
    acc_ref[...] += _dot(a, wd_ref[...])
    tail_ref[0] = g[tm - 8:, :]

    @pl.when(f == pl.num_programs(1) - 1)
    def _():
        out = h_ref[...] + acc_ref[...]
        if final_norm:
            out = _rms(out, gf_ref[...])
        o_ref[...] = out


def _ffn(h, g, wg, wu, cw, cb, wd, gf, seq_len, tm, tf, final_norm):
    m, d = h.shape
    ff = wg.shape[1]
    nt = m // tm
    hb = tm // FFN_HALO
    out, tail = pl.pallas_call(
        functools.partial(_ffn_kernel, tiles_per_seq=seq_len // tm, final_norm=final_norm),
        grid=(nt, ff // tf),
        in_specs=[
            pl.BlockSpec((tm, d), lambda i, f: (i, 0)),
            pl.BlockSpec((FFN_HALO, d), lambda i, f: (jnp.maximum(i * hb - 1, 0), 0)),
            pl.BlockSpec((1, d), lambda i, f: (0, 0)),
            pl.BlockSpec((d, tf), lambda i, f: (0, f)),
            pl.BlockSpec((d, tf), lambda i, f: (0, f)),
            pl.BlockSpec((CONV_W, tf), lambda i, f: (0, f)),
            pl.BlockSpec((1, tf), lambda i, f: (0, f)),
            pl.BlockSpec((tf, d), lambda i, f: (f, 0)),
            pl.BlockSpec((1, d), lambda i, f: (0, 0)),
        ],
        out_specs=[
            pl.BlockSpec((tm, d), lambda i, f: (i, 0)),
            pl.BlockSpec((1, 8, tf), lambda i, f: (i, 0, f)),
        ],
        out_shape=[jax.ShapeDtypeStruct((m, d), F32), jax.ShapeDtypeStruct((nt, 8, ff), F32)],
        scratch_shapes=[pltpu.VMEM((FFN_HALO + tm, d), BF16), pltpu.VMEM((FFN_HALO + tm, tf), F32),
                        pltpu.VMEM((tm, d), F32)],
        compiler_params=_params(("parallel", "arbitrary")),
        name="ffn",
    )(h, h, g, wg, wu, cw, cb, wd, gf)
    return out, tail


def _ffn_s_kernel(h_ref, b1_ref, b2_ref, g_ref, wg_ref, wu_ref, cw_ref, cb_ref, wd_ref, gf_ref,
                  o_ref, gout_ref, xn_ref, acc_ref, *, seq_len, final_norm):
    f = pl.program_id(0)

    @pl.when(f == 0)
    def _():
        xn_ref[...] = _rms(h_ref[...], g_ref[...]).astype(BF16)
        acc_ref[...] = jnp.zeros_like(acc_ref)

    xn = xn_ref[...]
    g = _dot(xn, wg_ref[...])
    t = lax.broadcasted_iota(jnp.int32, g.shape, 0) % seq_len
    p1 = jnp.where(t < 1, b1_ref[...], pltpu.roll(g, 1, 0))
    p2 = jnp.where(t < 2, b2_ref[...], pltpu.roll(g, 2, 0))
    cw = cw_ref[...]
    c = cb_ref[...] + p2 * cw[0:1] + p1 * cw[1:2] + g * cw[2:3]
    u = _dot(xn, wu_ref[...])
    a = (c * jax.nn.sigmoid(c) * u).astype(BF16)
    acc_ref[...] += _dot(a, wd_ref[...])
    gout_ref[...] = g

    @pl.when(f == pl.num_programs(0) - 1)
    def _():
        out = h_ref[...] + acc_ref[...]
        if final_norm:
            out = _rms(out, gf_ref[...])
        o_ref[...] = out


def _ffn_s(h, buf, g, wg, wu, cw, cb, wd, gf, seq_len, tf, final_norm):
    m, d = h.shape
    ff = wg.shape[1]
    nseq = m // seq_len
    zeros = jnp.zeros((nseq, seq_len - 2, ff), F32)
    b1 = jnp.concatenate([buf[:, 1:2], jnp.zeros((nseq, 1, ff), F32), zeros], axis=1).reshape(m, ff)
    b2 = jnp.concatenate([buf[:, 0:1], buf[:, 1:2], zeros], axis=1).reshape(m, ff)
    out, gout = pl.pallas_call(
        functools.partial(_ffn_s_kernel, seq_len=seq_len, final_norm=final_norm),
        grid=(ff // tf,),
        in_specs=[
            pl.BlockSpec((m, d), lambda f: (0, 0)),
            pl.BlockSpec((m, tf), lambda f: (0, f)),
            pl.BlockSpec((m, tf), lambda f: (0, f)),
            pl.BlockSpec((1, d), lambda f: (0, 0)),
            pl.BlockSpec((d, tf), lambda f: (0, f)),
            pl.BlockSpec((d, tf), lambda f: (0, f)),
            pl.BlockSpec((CONV_W, tf), lambda f: (0, f)),
            pl.BlockSpec((1, tf), lambda f: (0, f)),
            pl.BlockSpec((tf, d), lambda f: (f, 0)),
            pl.BlockSpec((1, d), lambda f: (0, 0)),
        ],
        out_specs=[
            pl.BlockSpec((m, d), lambda f: (0, 0)),
            pl.BlockSpec((m, tf), lambda f: (0, f)),
        ],
        out_shape=[jax.ShapeDtypeStruct((m, d), F32), jax.ShapeDtypeStruct((m, ff), F32)],
        scratch_shapes=[pltpu.VMEM((m, d), BF16), pltpu.VMEM((m, d), F32)],
        compiler_params=_params(("arbitrary",)),
        name="ffn_sample",
    )(h, b1, b2, g, wg, wu, cw, cb, wd, gf)
    return out, gout


SLAB = 2 * HEAD_DIM


def _heads_t(q, slot):
    b, l, nh, dd = q.shape
    keep = jnp.asarray(slot)[:, None] == jnp.arange(2)[None, :]
    qt = jnp.transpose(q, (0, 2, 3, 1))
    out = jnp.where(keep[None, :, :, None, None], qt[:, :, None, :, :], jnp.zeros((), q.dtype))
    return out.reshape(b, nh * SLAB, l)


REDUCE_WAYS = 8


def _rows_fold(x, reduce_fn):
    r, n = x.shape
    ways = REDUCE_WAYS if r % (8 * REDUCE_WAYS) == 0 else 1
    part = reduce_fn(x.reshape(r // (8 * ways), ways, 8, n), axis=0)
    return reduce_fn(part, axis=0)


def _flash_update_t(sts, vts, m_ref, l_ref, acc_ref, idx):
    m_prev = m_ref[idx]
    m_new = functools.reduce(jnp.maximum, [jnp.max(s, axis=0, keepdims=True) for s in sts], m_prev)
    alpha = jnp.exp2(m_prev - m_new)
    ps = [jnp.exp2(s - m_new) for s in sts]
    l_ref[idx] = alpha * l_ref[idx] + functools.reduce(jnp.add, [jnp.sum(p, axis=0, keepdims=True) for p in ps])
    acc = alpha * acc_ref[idx]
    for vt, p in zip(vts, ps):
        acc = acc + _dot(vt, p.astype(BF16))
    acc_ref[idx] = acc
    m_ref[idx] = m_new


def _flash_init(m_ref, l_ref, acc_ref):
    m_ref[...] = jnp.full_like(m_ref, NEG)
    l_ref[...] = jnp.zeros_like(l_ref)
    acc_ref[...] = jnp.zeros_like(acc_ref)


def _diff_kernel(lam_ref, qt_ref, k_ref, vt_ref, g_ref, o_ref, s_ref, m_ref, l_ref, acc_ref, *, tq, tk, out_scale):
    i = pl.program_id(1)
    jlast = (i * tq) // tk
    _flash_init(m_ref, l_ref, acc_ref)
    kpos = jlast * tk + lax.broadcasted_iota(jnp.int32, (tk, 2 * tq), 0)
    qpos = i * tq + lax.broadcasted_iota(jnp.int32, (tk, 2 * tq), 1) % tq
    ok = kpos <= qpos

    def scores(j, h, masked):
        q2 = jnp.concatenate([qt_ref[0, m * SLAB:(m + 1) * SLAB, :] for m in (2 * h, 2 * h + 1)], axis=1)
        s = _dot(k_ref[0, pl.ds(pl.multiple_of(j * tk, tk), tk), h * SLAB:(h + 1) * SLAB], q2)
        s_ref[h % 2] = jnp.where(ok, s, NEG) if masked else s

    def softmax_pv(j, h):
        _flash_update_t([s_ref[h % 2]], [vt_ref[0, j, h * A_VDIM:(h + 1) * A_VDIM, :]], m_ref, l_ref, acc_ref, h)

    def past_chunk(j, _):
        for h in range(A_HEADS):
            if h + 1 < A_HEADS:
                scores(j, h + 1, False)
            else:
                scores(jnp.minimum(j + 1, jlast - 1), 0, False)
            softmax_pv(j, h)
        return 0

    scores(0, 0, False)
    lax.fori_loop(0, jlast, past_chunk, 0)
    scores(jlast, 0, True)
    for h in range(A_HEADS):
        if h + 1 < A_HEADS:
            scores(jlast, h + 1, True)
        softmax_pv(jlast, h)

    lam = lam_ref[0]
    for h in range(A_HEADS):
        r = acc_ref[h] / l_ref[h]
        o = r[:, :tq] - lam * r[:, tq:]
        o = o * lax.rsqrt(jnp.mean(o * o, axis=0, keepdims=True) + EPS) * g_ref[...] * out_scale
        o_ref[0, h * A_VDIM:(h + 1) * A_VDIM, :] = o.astype(BF16)


def _diff_attn(lam, qt, qkv16, vt4, subln_col, out_scale, tq, tk):
    b, l, _ = qkv16.shape
    assert tk % tq == 0
    return pl.pallas_call(
        functools.partial(_diff_kernel, tq=tq, tk=tk, out_scale=out_scale),
        grid_spec=pltpu.PrefetchScalarGridSpec(
            num_scalar_prefetch=1,
            grid=(b, l // tq),
            in_specs=[
                pl.BlockSpec((1, 2 * A_HEADS * SLAB, tq), lambda bi, i, lam_ref: (bi, 0, i)),
                pl.BlockSpec((1, l, A_QK), lambda bi, i, lam_ref: (bi, 0, 1)),
                pl.BlockSpec((1, l // tk, A_V, tk), lambda bi, i, lam_ref: (bi, 0, 0, 0)),
                pl.BlockSpec((A_VDIM, 1), lambda bi, i, lam_ref: (0, 0)),
            ],
            out_specs=pl.BlockSpec((1, A_V, tq), lambda bi, i, lam_ref: (bi, 0, i)),
            scratch_shapes=[pltpu.VMEM((2, tk, 2 * tq), F32),
                            pltpu.VMEM((A_HEADS, 1, 2 * tq), F32), pltpu.VMEM((A_HEADS, 1, 2 * tq), F32),
                            pltpu.VMEM((A_HEADS, A_VDIM, 2 * tq), F32)],
        ),
        out_shape=jax.ShapeDtypeStruct((b, A_V, l), BF16),
        compiler_params=_params(("parallel", "arbitrary")),
        name="diff_attn",
    )(lam, qt, qkv16, vt4, subln_col)


def _blockmean_kernel(k_ref, o_ref):
    nb = o_ref.shape[1]
    for n in range(nb):
        o_ref[0, n:n + 1, :] = jnp.mean(k_ref[0, n * MOBA_BLOCK:(n + 1) * MOBA_BLOCK, :], axis=0, keepdims=True)


def _blockmean(x32, col_block, width, nb_step):
    b, l, _ = x32.shape
    nb = l // MOBA_BLOCK
    return pl.pallas_call(
        _blockmean_kernel,
        grid=(b, nb // nb_step),
        in_specs=[pl.BlockSpec((1, nb_step * MOBA_BLOCK, width), lambda bi, i: (bi, i, col_block))],
        out_specs=pl.BlockSpec((1, nb_step, width), lambda bi, i: (bi, i, 0)),
        out_shape=jax.ShapeDtypeStruct((b, nb, width), F32),
        compiler_params=_params(("parallel", "parallel")),
        name="blockmean",
    )(x32)


def _top3_select(gate, n_valid, axis=1):
    nb = gate.shape[axis]
    idx = lax.broadcasted_iota(jnp.int32, gate.shape, axis)
    sel = jnp.zeros(gate.shape, F32)
    for r in range(MOBA_TOPK):
        mx = jnp.max(gate, axis=axis, keepdims=True)
        first = jnp.min(jnp.where(gate == mx, idx, nb), axis=axis, keepdims=True)
        pick = idx == first
        sel = jnp.where(jnp.logical_and(pick, r < n_valid), 1.0, sel)
        gate = jnp.where(pick, -jnp.inf, gate)
    return sel


def _dot_nt_f32(a, b):
    return lax.dot_general(a, b, (((1,), (1,)), ((), ())), preferred_element_type=F32,
                           precision=lax.Precision.HIGHEST)


def _dot_f32(a, b):
    return jnp.dot(a, b, preferred_element_type=F32, precision=lax.Precision.HIGHEST)


def _moba_kernel(qt_ref, q32t_ref, k_ref, vt_ref, km_ref, o_ref, sel_ref, s_ref, m_ref, l_ref, acc_ref):
    i = pl.program_id(1)
    tq = MOBA_BLOCK
    nb = km_ref.shape[1]
    pairs = B_HEADS // 2

    _flash_init(m_ref, l_ref, acc_ref)
    blk = lax.broadcasted_iota(jnp.int32, (nb, tq), 0)
    for h in range(B_HEADS):
        rows = slice(h * HEAD_DIM, (h + 1) * HEAD_DIM)
        gate = _dot_f32(km_ref[0, :, rows], q32t_ref[0, rows, :])
        sel_ref[h] = _top3_select(jnp.where(blk < i, gate, -jnp.inf), i, axis=0)

    causal = (lax.broadcasted_iota(jnp.int32, (MOBA_BLOCK, 2 * tq), 0)
              <= lax.broadcasted_iota(jnp.int32, (MOBA_BLOCK, 2 * tq), 1) % tq)

    def scores(n, p, own):
        q2 = jnp.concatenate([qt_ref[0, h * SLAB:(h + 1) * SLAB, :] for h in (2 * p, 2 * p + 1)], axis=1)
        keys = pl.ds(pl.multiple_of(n * MOBA_BLOCK, MOBA_BLOCK), MOBA_BLOCK)
        s = _dot(k_ref[0, keys, p * SLAB:(p + 1) * SLAB], q2)
        if own:
            chosen = causal
        else:
            chosen = jnp.concatenate([sel_ref[h, pl.ds(n, 1), :] for h in (2 * p, 2 * p + 1)], axis=1) > 0.5
        s_ref[p % 2] = jnp.where(chosen, s, NEG)

    def softmax_pv(n, p):
        _flash_update_t([s_ref[p % 2]], [vt_ref[0, n, p * SLAB:(p + 1) * SLAB, :]], m_ref, l_ref, acc_ref, p)

    def past_block(n, _):
        for p in range(pairs):
            if p + 1 < pairs:
                scores(n, p + 1, False)
            else:
                scores(jnp.minimum(n + 1, i - 1), 0, False)
            softmax_pv(n, p)
        return 0

    scores(0, 0, False)
    lax.fori_loop(0, i, past_block, 0)
    scores(i, 0, True)
    for p in range(pairs):
        if p + 1 < pairs:
            scores(i, p + 1, True)
        softmax_pv(i, p)

    for h in range(B_HEADS):
        rows = slice((h % 2) * HEAD_DIM, (h % 2 + 1) * HEAD_DIM)
        cols = slice((h % 2) * tq, (h % 2 + 1) * tq)
        o_ref[0, h * HEAD_DIM:(h + 1) * HEAD_DIM, :] = (acc_ref[h // 2, rows, cols] / l_ref[h // 2, :, cols]).astype(BF16)


def _moba_attn(qt, q32t, qkv16, vt4, kmean):
    b, l, _ = qkv16.shape
    nb = l // MOBA_BLOCK
    kcol = (2 * A_QK + A_V + B_W) // B_W
    return pl.pallas_call(
        _moba_kernel,
        grid=(b, nb),
        in_specs=[
            pl.BlockSpec((1, B_HEADS * SLAB, MOBA_BLOCK), lambda bi, i: (bi, 0, i)),
            pl.BlockSpec((1, B_W, MOBA_BLOCK), lambda bi, i: (bi, 0, i)),
            pl.BlockSpec((1, l, B_W), lambda bi, i: (bi, 0, kcol)),
            pl.BlockSpec((1, nb, B_W, MOBA_BLOCK), lambda bi, i: (bi, 0, 0, 0)),
            pl.BlockSpec((1, nb, B_W), lambda bi, i: (bi, 0, 0)),
        ],
        out_specs=pl.BlockSpec((1, B_W, MOBA_BLOCK), lambda bi, i: (bi, 0, i)),
        out_shape=jax.ShapeDtypeStruct((b, B_W, l), BF16),
        scratch_shapes=[pltpu.VMEM((B_HEADS, nb, MOBA_BLOCK), F32),
                        pltpu.VMEM((2, MOBA_BLOCK, 2 * MOBA_BLOCK), F32),
                        pltpu.VMEM((B_HEADS // 2, 1, 2 * MOBA_BLOCK), F32),
                        pltpu.VMEM((B_HEADS // 2, 1, 2 * MOBA_BLOCK), F32),
                        pltpu.VMEM((B_HEADS // 2, SLAB, 2 * MOBA_BLOCK), F32)],
        compiler_params=_params(("parallel", "arbitrary")),
        name="moba_attn",
    )(qt, q32t, qkv16, vt4, kmean)


INT_MIN = -2 ** 31
NEG_BITS = int(np.float32(NEG).view(np.int32))


def _order_key(score):
    bits = pltpu.bitcast(jnp.where(score == 0.0, 0.0, score), jnp.int32)
    return bits ^ ((bits >> 31) & 0x7FFFFFFF)


SUBLANES = 8


def _fold(x, axis):
    if axis == 1:
        part = x[:, 0:LANES]
        for s in range(1, x.shape[1] // LANES):
            part = part + x[:, s * LANES:(s + 1) * LANES]
        return part
    return _rows_fold(x, jnp.sum)


def _count(key_ref, nch, preds, axis):
    shape = key_ref.shape[1:]
    part = (shape[0], LANES) if axis == 1 else (SUBLANES, shape[1])

    def body(c, accs):
        blk = key_ref[c]
        return tuple(a + _fold(jnp.where(p(blk), 1.0, 0.0), axis) for a, p in zip(accs, preds))

    accs = lax.fori_loop(0, nch, body, tuple(jnp.zeros(part, F32) for _ in preds))
    return tuple(jnp.sum(a, axis=axis, keepdims=True) for a in accs)


def _kth_largest_key(key_ref, nch, kk, axis):
    def cond(carry):
        it, _, _, n_ge = carry
        return jnp.logical_and(it < 32, jnp.max(n_ge - kk) > 0.5)

    def search(carry):
        it, t_u, bit, n_ge = carry
        cand_u = t_u | bit
        cand = cand_u ^ INT_MIN
        cnt, = _count(key_ref, nch, (lambda blk: blk >= cand,), axis)
        keep = cnt >= kk
        return it + 1, jnp.where(keep, cand_u, t_u), lax.shift_right_logical(bit, 1), jnp.where(keep, cnt, n_ge)

    n_all, = _count(key_ref, nch, (lambda blk: blk > INT_MIN,), axis)
    _, t_u, _, n_ge = lax.while_loop(
        cond, search, (jnp.int32(0), jnp.zeros(kk.shape, jnp.int32), jnp.full(kk.shape, INT_MIN, jnp.int32), n_all))
    return jnp.maximum(t_u ^ INT_MIN, INT_MIN + 1), n_ge


def _topk_bias(key_ref, out_ref, tri_ref, nch, kk, axis):
    tc = key_ref.shape[1 + axis]
    t, n_ge = _kth_largest_key(key_ref, nch, kk, axis)
    ties = jnp.max(n_ge - kk) > 0.5

    def put(c, take):
        if out_ref.dtype == jnp.int32:
            out_ref[c] = jnp.where(take, jnp.int32(0), jnp.int32(NEG_BITS))
        else:
            out_ref[c] = jnp.where(take, 0.0, NEG)

    @pl.when(jnp.logical_not(ties))
    def _():
        def body(c, _):
            put(c, key_ref[c] >= t)
            return 0
        lax.fori_loop(0, nch, body, 0)

    @pl.when(ties)
    def _():
        n_gt, = _count(key_ref, nch, (lambda blk: blk > t,), axis)
        need = kk - n_gt
        r = lax.broadcasted_iota(jnp.int32, (tc, tc), 0)
        cidx = lax.broadcasted_iota(jnp.int32, (tc, tc), 1)
        tri_ref[...] = jnp.where((r <= cidx) if axis == 1 else (cidx <= r), 1.0, 0.0).astype(BF16)

        def body(c, seen):
            blk = key_ref[c]
            eqm = blk == t
            eq16 = jnp.where(eqm, 1.0, 0.0).astype(BF16)
            rank = seen + (_dot(eq16, tri_ref[...]) if axis == 1 else _dot(tri_ref[...], eq16))
            put(c, jnp.logical_or(blk > t, jnp.logical_and(eqm, rank <= need)))
            return seen + jnp.sum(jnp.where(eqm, 1.0, 0.0), axis=axis, keepdims=True)
        lax.fori_loop(0, nch, body, jnp.zeros(kk.shape, F32))


DSA_SPLIT = 1


def _dsa_kernel(qt_ref, qit_ref, wit_ref, k_ref, vt_ref, ki_ref, o_ref,
                key_ref, tri_ref, s_ref, m_ref, l_ref, acc_ref, *, tq, tc, topk):
    i = pl.program_id(1)
    nch = ((i + 1) * tq - 1) // tc + 1
    qpos = i * tq + lax.broadcasted_iota(jnp.int32, (1, tq), 1)
    kk = jnp.minimum(topk, qpos + 1).astype(F32)

    def score_chunk(c, _):
        r0 = pl.multiple_of(c * tc, tc)
        kic = ki_ref[0, pl.ds(r0, tc), :]
        score = jnp.zeros((tc, tq), F32)
        for h in range(IDX_HEADS):
            lg = _dot(kic, qit_ref[0, h * SLAB:(h + 1) * SLAB, :])
            score = score + wit_ref[0, h:h + 1, :] * jnp.maximum(lg, 0.0)
        kpos = r0 + lax.broadcasted_iota(jnp.int32, (tc, tq), 0)
        key_ref[c] = jnp.where(kpos <= qpos, _order_key(score), INT_MIN)
        return 0

    lax.fori_loop(0, nch, score_chunk, 0)
    _topk_bias(key_ref, key_ref, tri_ref, nch, kk, axis=0)
    _flash_init(m_ref, l_ref, acc_ref)
    group = C_HEADS // C_KV_HEADS

    def scores(c, g):
        r0 = pl.multiple_of(c * tc, tc)
        qg = jnp.concatenate([qt_ref[0, h * SLAB:(h + 1) * SLAB, :] for h in range(g * group, (g + 1) * group)], axis=1)
        bias = jnp.concatenate([pltpu.bitcast(key_ref[c], F32)] * group, axis=1)
        s_ref[g % 2] = _dot(k_ref[0, pl.ds(r0, tc), (g // 2) * SLAB:(g // 2 + 1) * SLAB], qg) + bias

    def attend(c, _):
        for g in range(C_KV_HEADS):
            if g + 1 < C_KV_HEADS:
                scores(c, g + 1)
            else:
                scores(jnp.minimum(c + 1, nch - 1), 0)
            _flash_update_t([s_ref[g % 2]], [vt_ref[0, c, g * HEAD_DIM:(g + 1) * HEAD_DIM, :]], m_ref, l_ref, acc_ref, g)
        return 0

    scores(0, 0)
    lax.fori_loop(0, nch, attend, 0)
    for h in range(C_HEADS):
        g, hh = h // group, h % group
        cols = slice(hh * tq, (hh + 1) * tq)
        o_ref[0, h * HEAD_DIM:(h + 1) * HEAD_DIM, :] = (acc_ref[g, :, cols] / l_ref[g, :, cols]).astype(BF16)


def _dsa_attn(qt, qit, wit, qkv16, vt4, tq, tc):
    b, l, _ = qkv16.shape
    topk = min(DSA_TOPK, l // 4)
    nc = l // tc
    gq = (C_HEADS // C_KV_HEADS) * tq
    return pl.pallas_call(
        functools.partial(_dsa_kernel, tq=tq, tc=tc, topk=topk),
        grid=(b, l // tq),
        in_specs=[
            pl.BlockSpec((1, C_HEADS * SLAB, tq), lambda bi, i: (bi, 0, i)),
            pl.BlockSpec((1, IDX_HEADS * SLAB, tq), lambda bi, i: (bi, 0, i)),
            pl.BlockSpec((1, IDX_HEADS, tq), lambda bi, i: (bi, 0, i)),
            pl.BlockSpec((1, l, C_KV), lambda bi, i: (bi, 0, C_Q // C_KV)),
            pl.BlockSpec((1, nc, C_KV, tc), lambda bi, i: (bi, 0, 0, 0)),
            pl.BlockSpec((1, l, SLAB), lambda bi, i: (bi, 0, ODD_KI_COL // SLAB)),
        ],
        out_specs=pl.BlockSpec((1, C_Q, tq), lambda bi, i: (bi, 0, i)),
        out_shape=jax.ShapeDtypeStruct((b, C_Q, l), BF16),
        scratch_shapes=[pltpu.VMEM((nc, tc, tq), jnp.int32), pltpu.VMEM((tc, tc), BF16),
                        pltpu.VMEM((2, tc, gq), F32),
                        pltpu.VMEM((C_KV_HEADS, 1, gq), F32), pltpu.VMEM((C_KV_HEADS, 1, gq), F32),
                        pltpu.VMEM((C_KV_HEADS, HEAD_DIM, gq), F32)],
        compiler_params=_params(("parallel", "arbitrary")),
        name="dsa_attn",
    )(qt, qit, wit, qkv16, vt4, qkv16)


NEW_PAD = 16
EVEN_BLOCKS = 4
EVEN_PAGES = EVEN_BLOCKS * MOBA_BLOCK // PAGE_SIZE


def _fold_heads(x, n_groups, n_new):
    return jnp.sum(x.reshape(n_groups, n_new, x.shape[1]), axis=0)


def _dec_even_kernel(pt_ref, lam_ref, qbd_ref, *refs, n_new, n_blocks, out_scale):
    kpages = refs[:EVEN_PAGES]
    vpages = refs[EVEN_PAGES:2 * EVEN_PAGES]
    knew_ref, vnew_ref, g_ref, o_ref, ma_ref, la_ref, acca_ref, mb_ref, lb_ref, accb_ref, km_ref = refs[2 * EVEN_PAGES:]
    p = pl.program_id(1)
    ra = 8 * n_new
    qbd = qbd_ref[0]

    @pl.when(p == 0)
    def _():
        ma_ref[...] = jnp.full_like(ma_ref, NEG)
        la_ref[...] = jnp.zeros_like(la_ref)
        acca_ref[...] = jnp.zeros_like(acca_ref)
        km_ref[...] = jnp.zeros_like(km_ref)

    def attend(k16, v16, mask, blk):
        s = _dot_nt(qbd, k16)
        if mask is not None:
            s = jnp.where(mask, s, NEG)
        sa, sb = s[:ra], s[ra:]
        m_prev = ma_ref[...]
        m_new = jnp.maximum(m_prev, jnp.max(sa, axis=1, keepdims=True))
        alpha = jnp.exp2(m_prev - m_new)
        pa = jnp.exp2(sa - m_new)
        la_ref[...] = alpha * la_ref[...] + jnp.sum(pa, axis=1, keepdims=True)
        acca_ref[...] = alpha * acca_ref[...] + _dot(pa.astype(BF16), v16[:, :A_V])
        ma_ref[...] = m_new
        mb = jnp.max(sb, axis=1, keepdims=True)
        pb = jnp.exp2(sb - mb)
        mb_ref[blk] = mb
        lb_ref[blk] = jnp.sum(pb, axis=1, keepdims=True)
        accb_ref[blk] = _dot(pb.astype(BF16), v16[:, A_V:])

    per_block = MOBA_BLOCK // PAGE_SIZE
    n_steps = n_blocks // EVEN_BLOCKS

    @pl.when(p < n_steps)
    def _():
        for sb in range(EVEN_BLOCKS):
            pages = slice(sb * per_block, (sb + 1) * per_block)
            k32 = jnp.concatenate([r[...] for r in kpages[pages]], axis=0)
            v32 = jnp.concatenate([r[...] for r in vpages[pages]], axis=0)
            blk = p * EVEN_BLOCKS + sb
            km_ref[pl.ds(blk, 1), :] = jnp.mean(k32[:, A_QK:], axis=0, keepdims=True)
            attend(k32.astype(BF16), v32.astype(BF16), None, blk)

    @pl.when(p == n_steps)
    def _():
        rows = 16 * n_new
        t = lax.broadcasted_iota(jnp.int32, (rows, NEW_PAD), 0) % n_new
        j = lax.broadcasted_iota(jnp.int32, (rows, NEW_PAD), 1)
        attend(knew_ref[0], vnew_ref[0], j <= t, n_blocks)

        lam = lam_ref[0]
        r = acca_ref[...] / la_ref[...]
        m_idx = lax.broadcasted_iota(jnp.int32, r.shape, 0) // n_new
        col = lax.broadcasted_iota(jnp.int32, r.shape, 1)
        coef = jnp.where(m_idx % 2 == 0, 1.0, -lam)
        oa = _fold_heads(jnp.where(col // A_VDIM == m_idx // 2, r * coef, 0.0), 8, n_new)
        for h in range(A_HEADS):
            cs = slice(h * A_VDIM, (h + 1) * A_VDIM)
            o_ref[0, :, cs] = (_rms(oa[:, cs], g_ref[...]) * out_scale).astype(BF16)

        nbp = km_ref.shape[0]
        gate = _dot_nt_f32(qbd[ra:, A_QK:].astype(F32), km_ref[...])
        blk = lax.broadcasted_iota(jnp.int32, gate.shape, 1)
        sel = _top3_select(jnp.where(blk < n_blocks, gate, -jnp.inf), n_blocks)
        chosen = [sel[:, n:n + 1] > 0.5 for n in range(n_blocks)]
        m_all = mb_ref[n_blocks]
        for n in range(n_blocks):
            m_all = jnp.where(chosen[n], jnp.maximum(m_all, mb_ref[n]), m_all)
        w_own = jnp.exp2(mb_ref[n_blocks] - m_all)
        l_all, acc = w_own * lb_ref[n_blocks], w_own * accb_ref[n_blocks]
        for n in range(n_blocks):
            w = jnp.where(chosen[n], jnp.exp2(mb_ref[n] - m_all), 0.0)
            l_all = l_all + w * lb_ref[n]
            acc = acc + w * accb_ref[n]
        r = acc / l_all
        h_idx = lax.broadcasted_iota(jnp.int32, r.shape, 0) // n_new
        col = lax.broadcasted_iota(jnp.int32, r.shape, 1)
        ob = _fold_heads(jnp.where(col // HEAD_DIM == h_idx, r, 0.0), 8, n_new)
        o_ref[0, :, A_V:] = ob.astype(BF16)


def _block_diag_rows(q, n_maps, width):
    s, t, c = q.shape
    keep = (jnp.arange(c)[None, :] // width) == jnp.arange(n_maps)[:, None]
    return jnp.where(keep[None, :, None, :], q[:, None, :, :], 0).reshape(s, n_maps * t, c).astype(q.dtype)


def _dec_even(page_table, lam, qbd, cache_k, cache_v, layer, knew, vnew, subln_g, out_scale):
    nseq, n_pages = page_table.shape
    n_new = qbd.shape[1] // 16
    n_blocks = n_pages * PAGE_SIZE // MOBA_BLOCK
    n_steps = n_pages // EVEN_PAGES
    width = cache_k.shape[-1]

    def page_spec(r):
        return pl.BlockSpec((None, None, PAGE_SIZE, width),
                            lambda s, p, pt, lm: (pt[s, jnp.minimum(p, n_steps - 1) * EVEN_PAGES + r], layer, 0, 0))

    seq_spec = lambda shape: pl.BlockSpec((1,) + shape, lambda s, p, pt, lm: (s, 0, 0))
    ra = 8 * n_new
    nbp = -(-(n_blocks + 1) // 8) * 8
    return pl.pallas_call(
        functools.partial(_dec_even_kernel, n_new=n_new, n_blocks=n_blocks, out_scale=out_scale),
        grid_spec=pltpu.PrefetchScalarGridSpec(
            num_scalar_prefetch=2,
            grid=(nseq, n_steps + 1),
            in_specs=[seq_spec((16 * n_new, width))]
            + [page_spec(r) for r in range(EVEN_PAGES)] + [page_spec(r) for r in range(EVEN_PAGES)]
            + [seq_spec((NEW_PAD, width)), seq_spec((NEW_PAD, width)),
               pl.BlockSpec((1, A_VDIM), lambda s, p, pt, lm: (0, 0))],
            out_specs=seq_spec((n_new, A_V + B_W)),
            scratch_shapes=[pltpu.VMEM((ra, 1), F32), pltpu.VMEM((ra, 1), F32), pltpu.VMEM((ra, A_V), F32),
                            pltpu.VMEM((n_blocks + 1, ra, 1), F32), pltpu.VMEM((n_blocks + 1, ra, 1), F32),
                            pltpu.VMEM((n_blocks + 1, ra, B_W), F32), pltpu.VMEM((nbp, B_W), F32)],
        ),
        out_shape=jax.ShapeDtypeStruct((nseq, n_new, A_V + B_W), BF16),
        compiler_params=_params(("parallel", "arbitrary")),
        name="dec_even",
    )(page_table, lam, qbd, *([cache_k] * EVEN_PAGES), *([cache_v] * EVEN_PAGES), knew, vnew, subln_g)


ODD_PAGES = 16
ODD_CHUNK = ODD_PAGES * PAGE_SIZE


def _dec_select_kernel(pt_ref, qi_ref, wi_ref, *refs, n_new, n_chunks, topk):
    pages = refs[:ODD_PAGES]
    kinew_ref, bias_ref, key_ref, tri_ref = refs[ODD_PAGES:]
    p = pl.program_id(1)
    qi = qi_ref[0]
    wi = wi_ref[0]

    def scores(ki16):
        lg = _dot_nt(qi, ki16)
        return _fold_heads(wi * jnp.maximum(lg, 0.0), IDX_HEADS, n_new)

    @pl.when(p < n_chunks)
    def _():
        for r in range(ODD_PAGES):
            key_ref[p, :, r * PAGE_SIZE:(r + 1) * PAGE_SIZE] = _order_key(scores(pages[r][...].astype(BF16)))

    @pl.when(p == n_chunks)
    def _():
        sc = scores(kinew_ref[0])
        t = lax.broadcasted_iota(jnp.int32, sc.shape, 0)
        j = lax.broadcasted_iota(jnp.int32, sc.shape, 1)
        key_ref[n_chunks] = jnp.full((n_new, ODD_CHUNK), INT_MIN, jnp.int32)
        key_ref[n_chunks, :, 0:PAGE_SIZE] = jnp.where(j <= t, _order_key(sc), INT_MIN)
        qpos = n_chunks * ODD_CHUNK + lax.broadcasted_iota(jnp.int32, (n_new, 1), 0)
        kk = jnp.minimum(topk, qpos + 1).astype(F32)
        _topk_bias(key_ref, bias_ref.at[0], tri_ref, n_chunks + 1, kk, axis=1)


def _dec_select(page_table, qi, wi, cache_ki, layer, kinew):
    nseq, n_pages = page_table.shape
    n_new = qi.shape[1] // IDX_HEADS
    n_chunks = n_pages // ODD_PAGES
    topk = min(DSA_TOPK, (n_pages * PAGE_SIZE + n_new) // 4)

    def page_spec(r):
        return pl.BlockSpec((None, None, PAGE_SIZE, IDX_DIM),
                            lambda s, p, pt: (pt[s, jnp.minimum(p, n_chunks - 1) * ODD_PAGES + r], layer, 0, 0))

    return pl.pallas_call(
        functools.partial(_dec_select_kernel, n_new=n_new, n_chunks=n_chunks, topk=topk),
        grid_spec=pltpu.PrefetchScalarGridSpec(
            num_scalar_prefetch=1,
            grid=(nseq, n_chunks + 1),
            in_specs=[pl.BlockSpec((1, IDX_HEADS * n_new, IDX_DIM), lambda s, p, pt: (s, 0, 0)),
                      pl.BlockSpec((1, IDX_HEADS * n_new, 1), lambda s, p, pt: (s, 0, 0))]
            + [page_spec(r) for r in range(ODD_PAGES)]
            + [pl.BlockSpec((1, PAGE_SIZE, IDX_DIM), lambda s, p, pt: (s, 0, 0))],
            out_specs=pl.BlockSpec((1, n_chunks + 1, n_new, ODD_CHUNK), lambda s, p, pt: (s, 0, 0, 0)),
            scratch_shapes=[pltpu.VMEM((n_chunks + 1, n_new, ODD_CHUNK), jnp.int32),
                            pltpu.VMEM((ODD_CHUNK, ODD_CHUNK), BF16)],
        ),
        out_shape=jax.ShapeDtypeStruct((nseq, n_chunks + 1, n_new, ODD_CHUNK), F32),
        compiler_params=_params(("parallel", "arbitrary")),
        name="dec_select",
    )(page_table, qi, wi, *([cache_ki] * ODD_PAGES), kinew)


def _dec_odd_kernel(pt_ref, qbd_ref, bias_ref, *refs, n_new, n_chunks):
    kpages = refs[:ODD_PAGES]
    vpages = refs[ODD_PAGES:2 * ODD_PAGES]
    knew_ref, vnew_ref, o_ref, m_ref, l_ref, acc_ref = refs[2 * ODD_PAGES:]
    p = pl.program_id(1)
    qbd = qbd_ref[0]

    @pl.when(p == 0)
    def _():
        m_ref[...] = jnp.full_like(m_ref, NEG)
        l_ref[...] = jnp.zeros_like(l_ref)
        acc_ref[...] = jnp.zeros_like(acc_ref)

    def attend(k16, v16, bias):
        s = _dot_nt(qbd, k16) + jnp.tile(bias, (C_HEADS, 1))
        m_prev = m_ref[...]
        m_new = jnp.maximum(m_prev, jnp.max(s, axis=1, keepdims=True))
        alpha = jnp.exp2(m_prev - m_new)
        pr = jnp.exp2(s - m_new)
        l_ref[...] = alpha * l_ref[...] + jnp.sum(pr, axis=1, keepdims=True)
        acc_ref[...] = alpha * acc_ref[...] + _dot(pr.astype(BF16), v16)
        m_ref[...] = m_new

    @pl.when(p < n_chunks)
    def _():
        k16 = jnp.concatenate([r[...] for r in kpages], axis=0).astype(BF16)
        v16 = jnp.concatenate([r[...] for r in vpages], axis=0).astype(BF16)
        attend(k16, v16, bias_ref[0, 0])

    @pl.when(p == n_chunks)
    def _():
        attend(knew_ref[0], vnew_ref[0], bias_ref[0, 0, :, 0:NEW_PAD])
        r = acc_ref[...] / l_ref[...]
        group = C_HEADS // C_KV_HEADS
        for h in range(C_HEADS):
            g = h // group
            o_ref[0, :, h * HEAD_DIM:(h + 1) * HEAD_DIM] = (
                r[h * n_new:(h + 1) * n_new, g * HEAD_DIM:(g + 1) * HEAD_DIM].astype(BF16))


def _dec_odd(page_table, qbd, bias, cache_k, cache_v, layer, knew, vnew):
    nseq, n_pages = page_table.shape
    n_new = qbd.shape[1] // C_HEADS
    n_chunks = n_pages // ODD_PAGES

    def page_spec(r):
        return pl.BlockSpec((None, None, PAGE_SIZE, C_KV),
                            lambda s, p, pt: (pt[s, jnp.minimum(p, n_chunks - 1) * ODD_PAGES + r], layer, 0, 0))

    seq_spec = lambda shape: pl.BlockSpec((1,) + shape, lambda s, p, pt: (s, 0, 0))
    rows = C_HEADS * n_new
    return pl.pallas_call(
        functools.partial(_dec_odd_kernel, n_new=n_new, n_chunks=n_chunks),
        grid_spec=pltpu.PrefetchScalarGridSpec(
            num_scalar_prefetch=1,
            grid=(nseq, n_chunks + 1),
            in_specs=[seq_spec((rows, C_KV)),
                      pl.BlockSpec((1, 1, n_new, ODD_CHUNK), lambda s, p, pt: (s, p, 0, 0))]
            + [page_spec(r) for r in range(ODD_PAGES)] + [page_spec(r) for r in range(ODD_PAGES)]
            + [seq_spec((NEW_PAD, C_KV)), seq_spec((NEW_PAD, C_KV))],
            out_specs=seq_spec((n_new, C_Q)),
            scratch_shapes=[pltpu.VMEM((rows, 1), F32), pltpu.VMEM((rows, 1), F32), pltpu.VMEM((rows, C_KV), F32)],
        ),
        out_shape=jax.ShapeDtypeStruct((nseq, n_new, C_Q), BF16),
        compiler_params=_params(("parallel", "arbitrary")),
        name="dec_odd",
    )(page_table, qbd, bias, *([cache_k] * ODD_PAGES), *([cache_v] * ODD_PAGES), knew, vnew)


ROW_TILE = 1024
FFN_TF = 256


def _pad_rows(a, rows):
    return jnp.pad(a, ((0, 0), (0, rows - a.shape[1]), (0, 0)))


def _gqa_rows(q):
    s, t, nh, d = q.shape
    group = nh // C_KV_HEADS
    keep = (jnp.arange(nh)[:, None] // group) == jnp.arange(C_KV_HEADS)[None, :]
    qh = jnp.swapaxes(q, 1, 2)
    out = jnp.where(keep[None, :, None, :, None], qh[:, :, :, None, :], 0)
    return out.reshape(s, nh * t, C_KV_HEADS * d).astype(q.dtype)


def kernel(x_prompt, x_sample, cache_k_even, cache_v_even, cache_k_odd, cache_v_odd, cache_kidx_odd, state_conv,
           page_table, g_mix, g_ffn, g_final, w_in_even, w_out_even, lam_even, subln_even, w_in_odd, w_out_odd,
           w_gate, w_up, conv_w, conv_b, w_down):
    bsz, seq, d = x_prompt.shape
    nseq, n_new, _ = x_sample.shape
    depth = g_mix.shape[0]
    ff = w_gate.shape[-1]
    past_len = page_table.shape[1] * PAGE_SIZE
    bf = lambda a: a.astype(BF16)

    cos_p, sin_p = _rope_tables(jnp.arange(seq, dtype=jnp.int32))
    cos_s, sin_s = _rope_tables(jnp.tile(past_len + jnp.arange(n_new, dtype=jnp.int32), nseq))
    ecols, ocols = _even_cols(), _odd_cols()
    qb0 = 2 * A_QK + A_V
    tm = min(ROW_TILE, seq)
    ms = nseq * n_new

    hp = x_prompt.reshape(bsz * seq, d)
    hs = x_sample.reshape(ms, d)
    ke_p, ve_p, ko_p, vo_p, kio_p, cs_p = [], [], [], [], [], []
    ke_s, ve_s, ko_s, vo_s, kio_s, cs_s = [], [], [], [], [], []
    for layer in range(depth):
        e = layer // 2
        g = g_mix[layer][None]
        if layer % 2 == 0:
            w16, wo = bf(w_in_even[e]), bf(w_out_even[e])
            lam_init = 0.8 - 0.6 * math.exp(-0.3 * layer)
            lp = lam_even[e].astype(F32)
            lam = (jnp.exp(jnp.sum(lp[0] * lp[1])) - jnp.exp(jnp.sum(lp[2] * lp[3])) + lam_init).reshape(1)
            sg = subln_even[e][None]
            out_scale = 1.0 - lam_init

            p32, p16 = _inproj(hp, g, w16, cos_p, sin_p, *ecols, tm=tm)
            p32, p16 = p32.reshape(bsz, seq, -1), p16.reshape(bsz, seq, -1)
            pair = jnp.arange(2 * A_HEADS) % 2
            qat = _heads_t(p16[..., :A_QK].reshape(bsz, seq, 2 * A_HEADS, HEAD_DIM), pair)
            tk = min(512, seq)
            vat = jnp.swapaxes(p16[..., 2 * A_QK:qb0].reshape(bsz, seq // tk, tk, A_V), 2, 3)
            oat = _diff_attn(lam, qat, p16, vat, sg.reshape(A_VDIM, 1), out_scale, tq=256, tk=tk)
            kmean = _blockmean(p32, (qb0 + B_W) // B_W, B_W, nb_step=min(8, seq // MOBA_BLOCK))
            qbt = _heads_t(p16[..., qb0:qb0 + B_W].reshape(bsz, seq, B_HEADS, HEAD_DIM), pair)
            q32t = jnp.swapaxes(p32[..., qb0:qb0 + B_W], 1, 2)
            vbt = jnp.swapaxes(p16[..., qb0 + 2 * B_W:].reshape(bsz, seq // MOBA_BLOCK, MOBA_BLOCK, B_W), 2, 3)
            obt = _moba_attn(qbt, q32t, p16, vbt, kmean)
            o = jnp.swapaxes(jnp.concatenate([oat, obt], axis=1), 1, 2)
            hp = _outproj([o.reshape(bsz * seq, -1)], [wo], hp, tm=tm)
            ke_p.append(jnp.concatenate([p32[..., A_QK:2 * A_QK], p32[..., qb0 + B_W:qb0 + 2 * B_W]], -1))
            ve_p.append(jnp.concatenate([p32[..., 2 * A_QK:qb0], p32[..., qb0 + 2 * B_W:]], -1))

            s32, s16 = _inproj(hs, g, w16, cos_s, sin_s, *ecols, tm=ms)
            s32, s16 = s32.reshape(nseq, n_new, -1), s16.reshape(nseq, n_new, -1)
            qbd = _block_diag_rows(jnp.concatenate([s16[..., :A_QK], s16[..., qb0:qb0 + B_W]], -1), 16, HEAD_DIM)
            knew = jnp.concatenate([s16[..., A_QK:2 * A_QK], s16[..., qb0 + B_W:qb0 + 2 * B_W]], -1)
            vnew = jnp.concatenate([s16[..., 2 * A_QK:qb0], s16[..., qb0 + 2 * B_W:]], -1)
            o = _dec_even(page_table, lam, qbd, cache_k_even, cache_v_even, e,
                          _pad_rows(knew, NEW_PAD), _pad_rows(vnew, NEW_PAD), sg, out_scale)
            hs = _outproj([o.reshape(ms, -1)], [wo], hs, tm=ms)
            ke_s.append(jnp.concatenate([s32[..., A_QK:2 * A_QK], s32[..., qb0 + B_W:qb0 + 2 * B_W]], -1))
            ve_s.append(jnp.concatenate([s32[..., 2 * A_QK:qb0], s32[..., qb0 + 2 * B_W:]], -1))
        else:
            w16 = bf(jnp.pad(w_in_odd[e], ((0, 0), (0, ODD_NPAD - ODD_N))))
            wo = bf(w_out_odd[e])
            kc, vc, kic, wic = C_Q, C_Q + C_KV, ODD_KI_COL, ODD_KI_COL + IDX_DIM

            p32, p16 = _inproj(hp, g, w16, cos_p, sin_p, *ocols, tm=tm)
            p32, p16 = p32.reshape(bsz, seq, -1), p16.reshape(bsz, seq, -1)
            tc = min(512, seq)
            qt = _heads_t(p16[..., :C_Q].reshape(bsz, seq, C_HEADS, HEAD_DIM),
                          (jnp.arange(C_HEADS) // (C_HEADS // C_KV_HEADS)) % 2)
            qit = _heads_t(p16[..., vc + C_KV:kic].reshape(bsz, seq, IDX_HEADS, IDX_DIM), jnp.zeros(IDX_HEADS, jnp.int32))
            wit = jnp.swapaxes(p32[..., wic:wic + IDX_HEADS], 1, 2)
            vt4 = jnp.swapaxes(p16[..., vc:vc + C_KV].reshape(bsz, seq // tc, tc, C_KV), 2, 3)
            ot = _dsa_attn(qt, qit, wit, p16, vt4, tq=256, tc=tc)
            hp = _outproj([jnp.swapaxes(ot, 1, 2).reshape(bsz * seq, -1)], [wo], hp, tm=tm)
            ko_p.append(p32[..., kc:kc + C_KV])
            vo_p.append(p32[..., vc:vc + C_KV])
            kio_p.append(p32[..., kic:kic + IDX_DIM])

            s32, s16 = _inproj(hs, g, w16, cos_s, sin_s, *ocols, tm=ms)
            s32, s16 = s32.reshape(nseq, n_new, -1), s16.reshape(nseq, n_new, -1)
            qi = s16[..., vc + C_KV:kic].reshape(nseq, n_new, IDX_HEADS, IDX_DIM)
            qi = jnp.swapaxes(qi, 1, 2).reshape(nseq, IDX_HEADS * n_new, IDX_DIM)
            wi = jnp.swapaxes(s32[..., wic:wic + IDX_HEADS], 1, 2).reshape(nseq, IDX_HEADS * n_new, 1)
            bias = _dec_select(page_table, qi, wi, cache_kidx_odd, e, _pad_rows(s16[..., kic:kic + IDX_DIM], PAGE_SIZE))
            qbd = _gqa_rows(s16[..., :C_Q].reshape(nseq, n_new, C_HEADS, HEAD_DIM))
            o = _dec_odd(page_table, qbd, bias, cache_k_odd, cache_v_odd, e,
                         _pad_rows(s16[..., kc:kc + C_KV], NEW_PAD), _pad_rows(s16[..., vc:vc + C_KV], NEW_PAD))
            hs = _outproj([o.reshape(ms, -1)], [wo], hs, tm=ms)
            ko_s.append(s32[..., kc:kc + C_KV])
            vo_s.append(s32[..., vc:vc + C_KV])
            kio_s.append(s32[..., kic:kic + IDX_DIM])

        last = layer == depth - 1
        ffn_w = (g_ffn[layer][None], bf(w_gate[layer]), bf(w_up[layer]), conv_w[layer], conv_b[layer][None],
                 bf(w_down[layer]), g_final[None])
        hp, tail = _ffn(hp, *ffn_w, seq_len=seq, tm=tm, tf=FFN_TF, final_norm=last)
        cs_p.append(tail.reshape(bsz, seq // tm, 8, ff)[:, -1, 8 - (CONV_W - 1):])
        hs, gout = _ffn_s(hs, state_conv[layer], *ffn_w, seq_len=n_new, tf=FFN_TF, final_norm=last)
        cs_s.append(gout.reshape(nseq, n_new, ff)[:, n_new - (CONV_W - 1):])

    def to_pages(rows):
        r = jnp.stack(rows, 1)
        b, nl, s, w = r.shape
        return r.reshape(b, nl, s // PAGE_SIZE, PAGE_SIZE, w).transpose(0, 2, 1, 3, 4)

    return (hp.reshape(bsz, seq, d), hs.reshape(nseq, n_new, d),
            to_pages(ke_p), to_pages(ve_p), to_pages(ko_p), to_pages(vo_p), to_pages(kio_p), jnp.stack(cs_p, 0),
            jnp.stack(ke_s, 1), jnp.stack(ve_s, 1), jnp.stack(ko_s, 1), jnp.stack(vo_s, 1), jnp.stack(kio_s, 1),
            jnp.stack(cs_s, 0))
```

```python
import functools
import math

import jax
import jax.numpy as jnp
import numpy as np
from jax import lax
from jax.experimental import pallas as pl
from jax.experimental.pallas import tpu as pltpu

F32 = jnp.float32
BF16 = jnp.bfloat16

HEAD_DIM = 64
A_HEADS = 4
A_VDIM = 2 * HEAD_DIM
B_HEADS = 8
MOBA_BLOCK = 256
MOBA_TOPK = 3
C_HEADS = 16
C_KV_HEADS = 4
IDX_HEADS = 8
IDX_DIM = 64
DSA_TOPK = 256
CONV_W = 3
ROPE_THETA = 10000.0
EPS = 1e-6
PAGE_SIZE = 128
A_QK = A_HEADS * 2 * HEAD_DIM
A_V = A_HEADS * A_VDIM
B_W = B_HEADS * HEAD_DIM
C_Q = C_HEADS * HEAD_DIM
C_KV = C_KV_HEADS * HEAD_DIM
IDX_Q = IDX_HEADS * IDX_DIM
IDX_W_SCALE = IDX_Q ** -0.5
ATTN_SCALE = HEAD_DIM ** -0.5
Q_SCALE = ATTN_SCALE * math.log2(math.e)

LANES = 128
NEG = -1e30
VMEM_LIMIT = 56 * 1024 * 1024

EVEN_N = 2 * A_QK + A_V + 3 * B_W
ODD_N = C_Q + 2 * C_KV + IDX_Q + IDX_DIM + IDX_HEADS
ODD_NPAD = 2304
ODD_KI_COL = C_Q + 2 * C_KV + IDX_Q
PROJ_TN = 768


def _params(sem, vmem=VMEM_LIMIT):
    return pltpu.CompilerParams(dimension_semantics=sem, vmem_limit_bytes=vmem)


def _dot_nt(a, b):
    return lax.dot_general(a, b, (((1,), (1,)), ((), ())), preferred_element_type=F32)


def _dot(a, b):
    return jnp.dot(a, b, preferred_element_type=F32)


def _rms(x, g):
    return x * lax.rsqrt(jnp.mean(x * x, axis=-1, keepdims=True) + EPS) * g


def _inproj_kernel(x_ref, g_ref, w_ref, cos_ref, sin_ref, rmask_ref, s32_ref, s16_ref,
                   o32_ref, o16_ref, xn_ref):
    @pl.when(pl.program_id(1) == 0)
    def _():
        xn_ref[...] = _rms(x_ref[...], g_ref[...]).astype(BF16)

    y = _dot(xn_ref[...], w_ref[...])
    tn = y.shape[1]
    reps = tn // LANES
    cos = jnp.tile(cos_ref[...], (1, reps))
    sin = jnp.tile(sin_ref[...], (1, reps))
    lane = lax.broadcasted_iota(jnp.int32, y.shape, 1)
    first_half = (lane % HEAD_DIM) < (HEAD_DIM // 2)
    partner = jnp.where(first_half, pltpu.roll(y, tn - HEAD_DIM // 2, 1), pltpu.roll(y, HEAD_DIM // 2, 1))
    roped = y * cos + partner * sin
    out = jnp.where(rmask_ref[...] > 0.5, roped, y) * s32_ref[...]
    o32_ref[...] = out
    o16_ref[...] = (out * s16_ref[...]).astype(BF16)


def _inproj(x, g, w16, cos, sin, rmask, s32, s16, tm):
    m, d = x.shape
    n = w16.shape[1]
    tn = PROJ_TN
    tblocks = cos.shape[0] // tm
    return pl.pallas_call(
        _inproj_kernel,
        grid=(m // tm, n // tn),
        in_specs=[
            pl.BlockSpec((tm, d), lambda i, j: (i, 0)),
            pl.BlockSpec((1, d), lambda i, j: (0, 0)),
            pl.BlockSpec((d, tn), lambda i, j: (0, j)),
            pl.BlockSpec((tm, LANES), lambda i, j: (i % tblocks, 0)),
            pl.BlockSpec((tm, LANES), lambda i, j: (i % tblocks, 0)),
            pl.BlockSpec((1, tn), lambda i, j: (0, j)),
            pl.BlockSpec((1, tn), lambda i, j: (0, j)),
            pl.BlockSpec((1, tn), lambda i, j: (0, j)),
        ],
        out_specs=[
            pl.BlockSpec((tm, tn), lambda i, j: (i, j)),
            pl.BlockSpec((tm, tn), lambda i, j: (i, j)),
        ],
        out_shape=[jax.ShapeDtypeStruct((m, n), F32), jax.ShapeDtypeStruct((m, n), BF16)],
        scratch_shapes=[pltpu.VMEM((tm, d), BF16)],
        compiler_params=_params(("parallel", "arbitrary")),
        name="inproj",
    )(x, g, w16, cos, sin, rmask, s32, s16)


def _rope_tables(pos):
    half = HEAD_DIM // 2
    inv = ROPE_THETA ** (-jnp.arange(half, dtype=F32) * 2.0 / HEAD_DIM)
    ang = pos.astype(F32)[:, None] * inv[None, :]
    cos = jnp.tile(jnp.cos(ang), (1, LANES // half))
    sin = jnp.tile(jnp.concatenate([-jnp.sin(ang), jnp.sin(ang)], axis=1), (1, LANES // HEAD_DIM))
    return cos, sin


def _col_rows(n, rope_ranges, scale32, scale16):
    cols = jnp.arange(n)
    rmask = jnp.zeros((n,), F32)
    for lo, hi in rope_ranges:
        rmask = jnp.where((cols >= lo) & (cols < hi), 1.0, rmask)
    s32 = jnp.ones((n,), F32)
    for lo, hi, v in scale32:
        s32 = jnp.where((cols >= lo) & (cols < hi), v, s32)
    s16 = jnp.ones((n,), F32)
    for lo, hi, v in scale16:
        s16 = jnp.where((cols >= lo) & (cols < hi), v, s16)
    return rmask[None], s32[None], s16[None]


def _even_cols():
    o = A_QK + A_QK + A_V
    return _col_rows(EVEN_N, [(0, 2 * A_QK), (o, o + 2 * B_W)], [],
                     [(0, A_QK, Q_SCALE), (o, o + B_W, Q_SCALE)])


def _odd_cols():
    return _col_rows(ODD_NPAD, [(0, C_Q + C_KV), (C_Q + 2 * C_KV, ODD_KI_COL + IDX_DIM)],
                     [(ODD_KI_COL + IDX_DIM, ODD_N, IDX_W_SCALE)], [(0, C_Q, Q_SCALE)])


def _outproj_kernel(*refs, n_in):
    xs, ws, h_ref, o_ref = refs[:n_in], refs[n_in:2 * n_in], refs[2 * n_in], refs[2 * n_in + 1]
    acc = h_ref[...]
    for x_ref, w_ref in zip(xs, ws):
        acc = acc + _dot(x_ref[...], w_ref[...])
    o_ref[...] = acc


def _outproj(xs, ws, h, tm):
    m, d = h.shape
    n_in = len(xs)
    in_specs = [pl.BlockSpec((tm, x.shape[1]), lambda i: (i, 0)) for x in xs]
    in_specs += [pl.BlockSpec(w.shape, lambda i: (0, 0)) for w in ws]
    in_specs += [pl.BlockSpec((tm, d), lambda i: (i, 0))]
    return pl.pallas_call(
        functools.partial(_outproj_kernel, n_in=n_in),
        grid=(m // tm,),
        in_specs=in_specs,
        out_specs=pl.BlockSpec((tm, d), lambda i: (i, 0)),
        out_shape=jax.ShapeDtypeStruct((m, d), F32),
        compiler_params=_params(("parallel",)),
        name="outproj",
    )(*xs, *ws, h)


FFN_HALO = 16


def _ffn_kernel(h_ref, halo_ref, g_ref, wg_ref, wu_ref, cw_ref, cb_ref, wd_ref, gf_ref,
                o_ref, tail_ref, xn_ref, gext_ref, acc_ref, *, tiles_per_seq, final_norm):
    i, f = pl.program_id(0), pl.program_id(1)
    tm = h_ref.shape[0]

    @pl.when(f == 0)
    def _():
        xn_ref[0:FFN_HALO, :] = _rms(halo_ref[...], g_ref[...]).astype(BF16)
        xn_ref[FFN_HALO:, :] = _rms(h_ref[...], g_ref[...]).astype(BF16)
        acc_ref[...] = jnp.zeros_like(acc_ref)

    xn = xn_ref[...]
    gext = _dot(xn, wg_ref[...])
    first = (i % tiles_per_seq) == 0
    row = lax.broadcasted_iota(jnp.int32, gext.shape, 0)
    gext_ref[...] = jnp.where(jnp.logical_and(first, row < FFN_HALO), 0.0, gext)
    g = gext_ref[FFN_HALO:, :]
    p1 = gext_ref[pl.ds(FFN_HALO - 1, tm), :]
    p2 = gext_ref[pl.ds(FFN_HALO - 2, tm), :]
    cw = cw_ref[...]
    c = cb_ref[...] + p2 * cw[0:1] + p1 * cw[1:2] + g * cw[2:3]
    u = _dot(xn[FFN_HALO:], wu_ref[...])
    a = (c * jax.nn.sigmoid(c) * u).astype(BF16)
    acc_ref[...] += _dot(a, wd_ref[...])
    tail_ref[0] = g[tm - 8:, :]

    @pl.when(f == pl.num_programs(1) - 1)
    def _():
        out = h_ref[...] + acc_ref[...]
        if final_norm:
            out = _rms(out, gf_ref[...])
        o_ref[...] = out


def _ffn(h, g, wg, wu, cw, cb, wd, gf, seq_len, tm, tf, final_norm):
    m, d = h.shape
    ff = wg.shape[1]
    nt = m // tm
    hb = tm // FFN_HALO
    out, tail = pl.pallas_call(
        functools.partial(_ffn_kernel, tiles_per_seq=seq_len // tm, final_norm=final_norm),
        grid=(nt, ff // tf),
        in_specs=[
            pl.BlockSpec((tm, d), lambda i, f: (i, 0)),
            pl.BlockSpec((FFN_HALO, d), lambda i, f: (jnp.maximum(i * hb - 1, 0), 0)),
            pl.BlockSpec((1, d), lambda i, f: (0, 0)),
            pl.BlockSpec((d, tf), lambda i, f: (0, f)),
            pl.BlockSpec((d, tf), lambda i, f: (0, f)),
            pl.BlockSpec((CONV_W, tf), lambda i, f: (0, f)),
            pl.BlockSpec((1, tf), lambda i, f: (0, f)),
            pl.BlockSpec((tf, d), lambda i, f: (f, 0)),
            pl.BlockSpec((1, d), lambda i, f: (0, 0)),
        ],
        out_specs=[
            pl.BlockSpec((tm, d), lambda i, f: (i, 0)),
            pl.BlockSpec((1, 8, tf), lambda i, f: (i, 0, f)),
        ],
        out_shape=[jax.ShapeDtypeStruct((m, d), F32), jax.ShapeDtypeStruct((nt, 8, ff), F32)],
        scratch_shapes=[pltpu.VMEM((FFN_HALO + tm, d), BF16), pltpu.VMEM((FFN_HALO + tm, tf), F32),
                        pltpu.VMEM((tm, d), F32)],
        compiler_params=_params(("parallel", "arbitrary")),
        name="ffn",
    )(h, h, g, wg, wu, cw, cb, wd, gf)
    return out, tail


def _ffn_s_kernel(h_ref, b1_ref, b2_ref, g_ref, wg_ref, wu_ref, cw_ref, cb_ref, wd_ref, gf_ref,
                  o_ref, gout_ref, xn_ref, acc_ref, *, seq_len, final_norm):
    f = pl.program_id(0)

    @pl.when(f == 0)
    def _():
        xn_ref[...] = _rms(h_ref[...], g_ref[...]).astype(BF16)
        acc_ref[...] = jnp.zeros_like(acc_ref)

    xn = xn_ref[...]
    g = _dot(xn, wg_ref[...])
    t = lax.broadcasted_iota(jnp.int32, g.shape, 0) % seq_len
    p1 = jnp.where(t < 1, b1_ref[...], pltpu.roll(g, 1, 0))
    p2 = jnp.where(t < 2, b2_ref[...], pltpu.roll(g, 2, 0))
    cw = cw_ref[...]
    c = cb_ref[...] + p2 * cw[0:1] + p1 * cw[1:2] + g * cw[2:3]
    u = _dot(xn, wu_ref[...])
    a = (c * jax.nn.sigmoid(c) * u).astype(BF16)
    acc_ref[...] += _dot(a, wd_ref[...])
    gout_ref[...] = g

    @pl.when(f == pl.num_programs(0) - 1)
    def _():
        out = h_ref[...] + acc_ref[...]
        if final_norm:
            out = _rms(out, gf_ref[...])
        o_ref[...] = out


def _ffn_s(h, buf, g, wg, wu, cw, cb, wd, gf, seq_len, tf, final_norm):
    m, d = h.shape
    ff = wg.shape[1]
    nseq = m // seq_len
    zeros = jnp.zeros((nseq, seq_len - 2, ff), F32)
    b1 = jnp.concatenate([buf[:, 1:2], jnp.zeros((nseq, 1, ff), F32), zeros], axis=1).reshape(m, ff)
    b2 = jnp.concatenate([buf[:, 0:1], buf[:, 1:2], zeros], axis=1).reshape(m, ff)
    out, gout = pl.pallas_call(
        functools.partial(_ffn_s_kernel, seq_len=seq_len, final_norm=final_norm),
        grid=(ff // tf,),
        in_specs=[
            pl.BlockSpec((m, d), lambda f: (0, 0)),
            pl.BlockSpec((m, tf), lambda f: (0, f)),
            pl.BlockSpec((m, tf), lambda f: (0, f)),
            pl.BlockSpec((1, d), lambda f: (0, 0)),
            pl.BlockSpec((d, tf), lambda f: (0, f)),
            pl.BlockSpec((d, tf), lambda f: (0, f)),
            pl.BlockSpec((CONV_W, tf), lambda f: (0, f)),
            pl.BlockSpec((1, tf), lambda f: (0, f)),
            pl.BlockSpec((tf, d), lambda f: (f, 0)),
            pl.BlockSpec((1, d), lambda f: (0, 0)),
        ],
        out_specs=[
            pl.BlockSpec((m, d), lambda f: (0, 0)),
            pl.BlockSpec((m, tf), lambda f: (0, f)),
        ],
        out_shape=[jax.ShapeDtypeStruct((m, d), F32), jax.ShapeDtypeStruct((m, ff), F32)],
        scratch_shapes=[pltpu.VMEM((m, d), BF16), pltpu.VMEM((m, d), F32)],
        compiler_params=_params(("arbitrary",)),
        name="ffn_sample",
    )(h, b1, b2, g, wg, wu, cw, cb, wd, gf)
    return out, gout


SLAB = 2 * HEAD_DIM


def _heads_t(q, slot):
    b, l, nh, dd = q.shape
    keep = jnp.asarray(slot)[:, None] == jnp.arange(2)[None, :]
    qt = jnp.transpose(q, (0, 2, 3, 1))
    out = jnp.where(keep[None, :, :, None, None], qt[:, :, None, :, :], jnp.zeros((), q.dtype))
    return out.reshape(b, nh * SLAB, l)


REDUCE_WAYS = 8


def _rows_fold(x, reduce_fn):
    r, n = x.shape
    ways = REDUCE_WAYS if r % (SUBLANES * REDUCE_WAYS) == 0 else 1
    part = reduce_fn(x.reshape(r // (SUBLANES * ways), ways, SUBLANES, n), axis=0)
    return reduce_fn(part, axis=0)


def _flash_update_t(sts, vts, m_ref, l_ref, acc_ref, idx):
    m_prev = m_ref[idx]
    m_new = functools.reduce(jnp.maximum, [jnp.max(s, axis=0, keepdims=True) for s in sts], m_prev)
    alpha = jnp.exp2(m_prev - m_new)
    ps = [jnp.exp2(s - m_new) for s in sts]
    l_ref[idx] = alpha * l_ref[idx] + functools.reduce(jnp.add, [jnp.sum(p, axis=0, keepdims=True) for p in ps])
    acc = alpha * acc_ref[idx]
    for vt, p in zip(vts, ps):
        acc = acc + _dot(vt, p.astype(BF16))
    acc_ref[idx] = acc
    m_ref[idx] = m_new


def _flash_init(m_ref, l_ref, acc_ref):
    m_ref[...] = jnp.full_like(m_ref, NEG)
    l_ref[...] = jnp.zeros_like(l_ref)
    acc_ref[...] = jnp.zeros_like(acc_ref)


def _diff_kernel(lam_ref, qt_ref, k_ref, vt_ref, g_ref, o_ref, s_ref, m_ref, l_ref, acc_ref, *, tq, tk, out_scale):
    i = pl.program_id(1)
    jlast = (i * tq) // tk
    _flash_init(m_ref, l_ref, acc_ref)
    kpos = jlast * tk + lax.broadcasted_iota(jnp.int32, (tk, 2 * tq), 0)
    qpos = i * tq + lax.broadcasted_iota(jnp.int32, (tk, 2 * tq), 1) % tq
    ok = kpos <= qpos

    def scores(j, h, masked):
        q2 = jnp.concatenate([qt_ref[0, m * SLAB:(m + 1) * SLAB, :] for m in (2 * h, 2 * h + 1)], axis=1)
        s = _dot(k_ref[0, pl.ds(pl.multiple_of(j * tk, tk), tk), h * SLAB:(h + 1) * SLAB], q2)
        s_ref[h % 2] = jnp.where(ok, s, NEG) if masked else s

    def softmax_pv(j, h):
        _flash_update_t([s_ref[h % 2]], [vt_ref[0, j, h * A_VDIM:(h + 1) * A_VDIM, :]], m_ref, l_ref, acc_ref, h)

    def past_chunk(j, _):
        for h in range(A_HEADS):
            if h + 1 < A_HEADS:
                scores(j, h + 1, False)
            else:
                scores(jnp.minimum(j + 1, jlast - 1), 0, False)
            softmax_pv(j, h)
        return 0

    scores(0, 0, False)
    lax.fori_loop(0, jlast, past_chunk, 0)
    scores(jlast, 0, True)
    for h in range(A_HEADS):
        if h + 1 < A_HEADS:
            scores(jlast, h + 1, True)
        softmax_pv(jlast, h)

    lam = lam_ref[0]
    for h in range(A_HEADS):
        r = acc_ref[h] / l_ref[h]
        o = r[:, :tq] - lam * r[:, tq:]
        o = o * lax.rsqrt(jnp.mean(o * o, axis=0, keepdims=True) + EPS) * g_ref[...] * out_scale
        o_ref[0, h * A_VDIM:(h + 1) * A_VDIM, :] = o.astype(BF16)


def _diff_attn(lam, qt, qkv16, vt4, subln_col, out_scale, tq, tk):
    b, l, _ = qkv16.shape
    assert tk % tq == 0
    return pl.pallas_call(
        functools.partial(_diff_kernel, tq=tq, tk=tk, out_scale=out_scale),
        grid_spec=pltpu.PrefetchScalarGridSpec(
            num_scalar_prefetch=1,
            grid=(b, l // tq),
            in_specs=[
                pl.BlockSpec((1, 2 * A_HEADS * SLAB, tq), lambda bi, i, lam_ref: (bi, 0, i)),
                pl.BlockSpec((1, l, A_QK), lambda bi, i, lam_ref: (bi, 0, 1)),
                pl.BlockSpec((1, l // tk, A_V, tk), lambda bi, i, lam_ref: (bi, 0, 0, 0)),
                pl.BlockSpec((A_VDIM, 1), lambda bi, i, lam_ref: (0, 0)),
            ],
            out_specs=pl.BlockSpec((1, A_V, tq), lambda bi, i, lam_ref: (bi, 0, i)),
            scratch_shapes=[pltpu.VMEM((2, tk, 2 * tq), F32),
                            pltpu.VMEM((A_HEADS, 1, 2 * tq), F32), pltpu.VMEM((A_HEADS, 1, 2 * tq), F32),
                            pltpu.VMEM((A_HEADS, A_VDIM, 2 * tq), F32)],
        ),
        out_shape=jax.ShapeDtypeStruct((b, A_V, l), BF16),
        compiler_params=_params(("parallel", "arbitrary")),
        name="diff_attn",
    )(lam, qt, qkv16, vt4, subln_col)


def _blockmean_kernel(k_ref, o_ref):
    nb = o_ref.shape[1]
    for n in range(nb):
        o_ref[0, n:n + 1, :] = jnp.mean(k_ref[0, n * MOBA_BLOCK:(n + 1) * MOBA_BLOCK, :], axis=0, keepdims=True)


def _blockmean(x32, col_block, width, nb_step):
    b, l, _ = x32.shape
    nb = l // MOBA_BLOCK
    return pl.pallas_call(
        _blockmean_kernel,
        grid=(b, nb // nb_step),
        in_specs=[pl.BlockSpec((1, nb_step * MOBA_BLOCK, width), lambda bi, i: (bi, i, col_block))],
        out_specs=pl.BlockSpec((1, nb_step, width), lambda bi, i: (bi, i, 0)),
        out_shape=jax.ShapeDtypeStruct((b, nb, width), F32),
        compiler_params=_params(("parallel", "parallel")),
        name="blockmean",
    )(x32)


def _top3_select(gate, n_valid, axis=1):
    nb = gate.shape[axis]
    idx = lax.broadcasted_iota(jnp.int32, gate.shape, axis)
    sel = jnp.zeros(gate.shape, F32)
    for r in range(MOBA_TOPK):
        mx = jnp.max(gate, axis=axis, keepdims=True)
        first = jnp.min(jnp.where(gate == mx, idx, nb), axis=axis, keepdims=True)
        pick = idx == first
        sel = jnp.where(jnp.logical_and(pick, r < n_valid), 1.0, sel)
        gate = jnp.where(pick, -jnp.inf, gate)
    return sel


def _dot_nt_f32(a, b):
    return lax.dot_general(a, b, (((1,), (1,)), ((), ())), preferred_element_type=F32,
                           precision=lax.Precision.HIGHEST)


def _dot_f32(a, b):
    return jnp.dot(a, b, preferred_element_type=F32, precision=lax.Precision.HIGHEST)


def _moba_kernel(qt_ref, q32t_ref, k_ref, vt_ref, km_ref, o_ref, sel_ref, s_ref, m_ref, l_ref, acc_ref):
    i = pl.program_id(1)
    tq = MOBA_BLOCK
    nb = km_ref.shape[1]
    pairs = B_HEADS // 2

    _flash_init(m_ref, l_ref, acc_ref)
    blk = lax.broadcasted_iota(jnp.int32, (nb, tq), 0)
    for h in range(B_HEADS):
        rows = slice(h * HEAD_DIM, (h + 1) * HEAD_DIM)
        gate = _dot_f32(km_ref[0, :, rows], q32t_ref[0, rows, :])
        sel_ref[h] = _top3_select(jnp.where(blk < i, gate, -jnp.inf), i, axis=0)

    causal = (lax.broadcasted_iota(jnp.int32, (MOBA_BLOCK, 2 * tq), 0)
              <= lax.broadcasted_iota(jnp.int32, (MOBA_BLOCK, 2 * tq), 1) % tq)

    def scores(n, p, own):
        q2 = jnp.concatenate([qt_ref[0, h * SLAB:(h + 1) * SLAB, :] for h in (2 * p, 2 * p + 1)], axis=1)
        keys = pl.ds(pl.multiple_of(n * MOBA_BLOCK, MOBA_BLOCK), MOBA_BLOCK)
        s = _dot(k_ref[0, keys, p * SLAB:(p + 1) * SLAB], q2)
        if own:
            chosen = causal
        else:
            chosen = jnp.concatenate([sel_ref[h, pl.ds(n, 1), :] for h in (2 * p, 2 * p + 1)], axis=1) > 0.5
        s_ref[p % 2] = jnp.where(chosen, s, NEG)

    def softmax_pv(n, p):
        _flash_update_t([s_ref[p % 2]], [vt_ref[0, n, p * SLAB:(p + 1) * SLAB, :]], m_ref, l_ref, acc_ref, p)

    def past_block(n, _):
        for p in range(pairs):
            if p + 1 < pairs:
                scores(n, p + 1, False)
            else:
                scores(jnp.minimum(n + 1, i - 1), 0, False)
            softmax_pv(n, p)
        return 0

    scores(0, 0, False)
    lax.fori_loop(0, i, past_block, 0)
    scores(i, 0, True)
    for p in range(pairs):
        if p + 1 < pairs:
            scores(i, p + 1, True)
        softmax_pv(i, p)

    for h in range(B_HEADS):
        rows = slice((h % 2) * HEAD_DIM, (h % 2 + 1) * HEAD_DIM)
        cols = slice((h % 2) * tq, (h % 2 + 1) * tq)
        o_ref[0, h * HEAD_DIM:(h + 1) * HEAD_DIM, :] = (acc_ref[h // 2, rows, cols] / l_ref[h // 2, :, cols]).astype(BF16)


def _moba_attn(qt, q32t, qkv16, vt4, kmean):
    b, l, _ = qkv16.shape
    nb = l // MOBA_BLOCK
    kcol = (2 * A_QK + A_V + B_W) // B_W
    return pl.pallas_call(
        _moba_kernel,
        grid=(b, nb),
        in_specs=[
            pl.BlockSpec((1, B_HEADS * SLAB, MOBA_BLOCK), lambda bi, i: (bi, 0, i)),
            pl.BlockSpec((1, B_W, MOBA_BLOCK), lambda bi, i: (bi, 0, i)),
            pl.BlockSpec((1, l, B_W), lambda bi, i: (bi, 0, kcol)),
            pl.BlockSpec((1, nb, B_W, MOBA_BLOCK), lambda bi, i: (bi, 0, 0, 0)),
            pl.BlockSpec((1, nb, B_W), lambda bi, i: (bi, 0, 0)),
        ],
        out_specs=pl.BlockSpec((1, B_W, MOBA_BLOCK), lambda bi, i: (bi, 0, i)),
        out_shape=jax.ShapeDtypeStruct((b, B_W, l), BF16),
        scratch_shapes=[pltpu.VMEM((B_HEADS, nb, MOBA_BLOCK), F32),
                        pltpu.VMEM((2, MOBA_BLOCK, 2 * MOBA_BLOCK), F32),
                        pltpu.VMEM((B_HEADS // 2, 1, 2 * MOBA_BLOCK), F32),
                        pltpu.VMEM((B_HEADS // 2, 1, 2 * MOBA_BLOCK), F32),
                        pltpu.VMEM((B_HEADS // 2, SLAB, 2 * MOBA_BLOCK), F32)],
        compiler_params=_params(("parallel", "arbitrary")),
        name="moba_attn",
    )(qt, q32t, qkv16, vt4, kmean)


INT_MIN = -2 ** 31
NEG_BITS = int(np.float32(NEG).view(np.int32))


def _order_key(score):
    bits = pltpu.bitcast(jnp.where(score == 0.0, 0.0, score), jnp.int32)
    return bits ^ ((bits >> 31) & 0x7FFFFFFF)


SUBLANES = 8


def _fold(x, axis):
    if axis == 1:
        part = x[:, 0:LANES]
        for s in range(1, x.shape[1] // LANES):
            part = part + x[:, s * LANES:(s + 1) * LANES]
        return part
    return _rows_fold(x, jnp.sum)


def _count(key_ref, nch, preds, axis):
    shape = key_ref.shape[1:]
    part = (shape[0], LANES) if axis == 1 else (SUBLANES, shape[1])

    def body(c, accs):
        blk = key_ref[c]
        return tuple(a + _fold(jnp.where(p(blk), 1.0, 0.0), axis) for a, p in zip(accs, preds))

    accs = lax.fori_loop(0, nch, body, tuple(jnp.zeros(part, F32) for _ in preds))
    return tuple(jnp.sum(a, axis=axis, keepdims=True) for a in accs)


def _kth_largest_key(key_ref, nch, kk, n_valid, axis):
    def cond(carry):
        it, _, _, n_ge = carry
        return jnp.logical_and(it < 32, jnp.max(n_ge - kk) > 0.5)

    def search(carry):
        it, t_u, bit, n_ge = carry
        cand_u = t_u | bit
        cand = cand_u ^ INT_MIN
        cnt, = _count(key_ref, nch, (lambda blk: blk >= cand,), axis)
        keep = cnt >= kk
        return it + 1, jnp.where(keep, cand_u, t_u), lax.shift_right_logical(bit, 1), jnp.where(keep, cnt, n_ge)

    _, t_u, _, n_ge = lax.while_loop(
        cond, search, (jnp.int32(0), jnp.zeros(kk.shape, jnp.int32), jnp.full(kk.shape, INT_MIN, jnp.int32), n_valid))
    return jnp.maximum(t_u ^ INT_MIN, INT_MIN + 1), n_ge


def _topk_bias(key_ref, out_ref, tri_ref, nch, kk, n_valid, axis):
    tc = key_ref.shape[1 + axis]
    t, n_ge = _kth_largest_key(key_ref, nch, kk, n_valid, axis)
    ties = jnp.max(n_ge - kk) > 0.5

    def put(c, take):
        if out_ref.dtype == jnp.int32:
            out_ref[c] = jnp.where(take, jnp.int32(0), jnp.int32(NEG_BITS))
        else:
            out_ref[c] = jnp.where(take, 0.0, NEG)

    @pl.when(jnp.logical_not(ties))
    def _():
        def body(c, _):
            put(c, key_ref[c] >= t)
            return 0
        lax.fori_loop(0, nch, body, 0)

    @pl.when(ties)
    def _():
        n_gt, = _count(key_ref, nch, (lambda blk: blk > t,), axis)
        need = kk - n_gt
        r = lax.broadcasted_iota(jnp.int32, (tc, tc), 0)
        cidx = lax.broadcasted_iota(jnp.int32, (tc, tc), 1)
        tri_ref[...] = jnp.where((r <= cidx) if axis == 1 else (cidx <= r), 1.0, 0.0).astype(BF16)

        def body(c, seen):
            blk = key_ref[c]
            eqm = blk == t
            eq16 = jnp.where(eqm, 1.0, 0.0).astype(BF16)
            rank = seen + (_dot(eq16, tri_ref[...]) if axis == 1 else _dot(tri_ref[...], eq16))
            put(c, jnp.logical_or(blk > t, jnp.logical_and(eqm, rank <= need)))
            return seen + jnp.sum(jnp.where(eqm, 1.0, 0.0), axis=axis, keepdims=True)
        lax.fori_loop(0, nch, body, jnp.zeros(kk.shape, F32))


def _dsa_kernel(qt_ref, qit_ref, wit_ref, k_ref, vt_ref, ki_ref, o_ref,
                key_ref, tri_ref, s_ref, m_ref, l_ref, acc_ref, *, tq, tc, topk):
    i = pl.program_id(1)
    nch = ((i + 1) * tq - 1) // tc + 1
    qpos = i * tq + lax.broadcasted_iota(jnp.int32, (1, tq), 1)
    kk = jnp.minimum(topk, qpos + 1).astype(F32)

    def score_chunk(c, _):
        r0 = pl.multiple_of(c * tc, tc)
        kic = ki_ref[0, pl.ds(r0, tc), :]
        score = jnp.zeros((tc, tq), F32)
        for h in range(IDX_HEADS):
            lg = _dot(kic, qit_ref[0, h * SLAB:(h + 1) * SLAB, :])
            score = score + wit_ref[0, h:h + 1, :] * jnp.maximum(lg, 0.0)
        kpos = r0 + lax.broadcasted_iota(jnp.int32, (tc, tq), 0)
        key_ref[c] = jnp.where(kpos <= qpos, _order_key(score), INT_MIN)
        return 0

    lax.fori_loop(0, nch, score_chunk, 0)
    _topk_bias(key_ref, key_ref, tri_ref, nch, kk, (qpos + 1).astype(F32), axis=0)
    _flash_init(m_ref, l_ref, acc_ref)
    group = C_HEADS // C_KV_HEADS

    def scores(c, g):
        r0 = pl.multiple_of(c * tc, tc)
        qg = jnp.concatenate([qt_ref[0, h * SLAB:(h + 1) * SLAB, :] for h in range(g * group, (g + 1) * group)], axis=1)
        bias = jnp.concatenate([pltpu.bitcast(key_ref[c], F32)] * group, axis=1)
        s_ref[g % 2] = _dot(k_ref[0, pl.ds(r0, tc), (g // 2) * SLAB:(g // 2 + 1) * SLAB], qg) + bias

    def attend(c, _):
        for g in range(C_KV_HEADS):
            if g + 1 < C_KV_HEADS:
                scores(c, g + 1)
            else:
                scores(jnp.minimum(c + 1, nch - 1), 0)
            _flash_update_t([s_ref[g % 2]], [vt_ref[0, c, g * HEAD_DIM:(g + 1) * HEAD_DIM, :]], m_ref, l_ref, acc_ref, g)
        return 0

    scores(0, 0)
    lax.fori_loop(0, nch, attend, 0)
    for h in range(C_HEADS):
        g, hh = h // group, h % group
        cols = slice(hh * tq, (hh + 1) * tq)
        o_ref[0, h * HEAD_DIM:(h + 1) * HEAD_DIM, :] = (acc_ref[g, :, cols] / l_ref[g, :, cols]).astype(BF16)


def _dsa_attn(qt, qit, wit, qkv16, vt4, tq, tc):
    b, l, _ = qkv16.shape
    topk = min(DSA_TOPK, l // 4)
    nc = l // tc
    gq = (C_HEADS // C_KV_HEADS) * tq
    return pl.pallas_call(
        functools.partial(_dsa_kernel, tq=tq, tc=tc, topk=topk),
        grid=(b, l // tq),
        in_specs=[
            pl.BlockSpec((1, C_HEADS * SLAB, tq), lambda bi, i: (bi, 0, i)),
            pl.BlockSpec((1, IDX_HEADS * SLAB, tq), lambda bi, i: (bi, 0, i)),
            pl.BlockSpec((1, IDX_HEADS, tq), lambda bi, i: (bi, 0, i)),
            pl.BlockSpec((1, l, C_KV), lambda bi, i: (bi, 0, C_Q // C_KV)),
            pl.BlockSpec((1, nc, C_KV, tc), lambda bi, i: (bi, 0, 0, 0)),
            pl.BlockSpec((1, l, SLAB), lambda bi, i: (bi, 0, ODD_KI_COL // SLAB)),
        ],
        out_specs=pl.BlockSpec((1, C_Q, tq), lambda bi, i: (bi, 0, i)),
        out_shape=jax.ShapeDtypeStruct((b, C_Q, l), BF16),
        scratch_shapes=[pltpu.VMEM((nc, tc, tq), jnp.int32), pltpu.VMEM((tc, tc), BF16),
                        pltpu.VMEM((2, tc, gq), F32),
                        pltpu.VMEM((C_KV_HEADS, 1, gq), F32), pltpu.VMEM((C_KV_HEADS, 1, gq), F32),
                        pltpu.VMEM((C_KV_HEADS, HEAD_DIM, gq), F32)],
        compiler_params=_params(("parallel", "arbitrary")),
        name="dsa_attn",
    )(qt, qit, wit, qkv16, vt4, qkv16)


NEW_PAD = 16
EVEN_BLOCKS = 8
EVEN_PAGES = EVEN_BLOCKS * MOBA_BLOCK // PAGE_SIZE


def _fold_heads(x, n_groups, n_new):
    return jnp.sum(x.reshape(n_groups, n_new, x.shape[1]), axis=0)


def _dec_even_kernel(pt_ref, lam_ref, qbd_ref, *refs, n_new, n_blocks, out_scale):
    kpages = refs[:EVEN_PAGES]
    vpages = refs[EVEN_PAGES:2 * EVEN_PAGES]
    knew_ref, vnew_ref, g_ref, o_ref, ma_ref, la_ref, acca_ref, mb_ref, lb_ref, accb_ref, km_ref = refs[2 * EVEN_PAGES:]
    p = pl.program_id(1)
    ra = 8 * n_new
    qbd = qbd_ref[0]

    @pl.when(p == 0)
    def _():
        ma_ref[...] = jnp.full_like(ma_ref, NEG)
        la_ref[...] = jnp.zeros_like(la_ref)
        acca_ref[...] = jnp.zeros_like(acca_ref)
        km_ref[...] = jnp.zeros_like(km_ref)

    def attend(k16, v16, mask, blk):
        s = _dot_nt(qbd, k16)
        if mask is not None:
            s = jnp.where(mask, s, NEG)
        sa, sb = s[:ra], s[ra:]
        m_prev = ma_ref[...]
        m_new = jnp.maximum(m_prev, jnp.max(sa, axis=1, keepdims=True))
        alpha = jnp.exp2(m_prev - m_new)
        pa = jnp.exp2(sa - m_new)
        la_ref[...] = alpha * la_ref[...] + jnp.sum(pa, axis=1, keepdims=True)
        acca_ref[...] = alpha * acca_ref[...] + _dot(pa.astype(BF16), v16[:, :A_V])
        ma_ref[...] = m_new
        mb = jnp.max(sb, axis=1, keepdims=True)
        pb = jnp.exp2(sb - mb)
        mb_ref[blk] = mb
        lb_ref[blk] = jnp.sum(pb, axis=1, keepdims=True)
        accb_ref[blk] = _dot(pb.astype(BF16), v16[:, A_V:])

    per_block = MOBA_BLOCK // PAGE_SIZE
    n_steps = n_blocks // EVEN_BLOCKS

    @pl.when(p < n_steps)
    def _():
        for sb in range(EVEN_BLOCKS):
            pages = slice(sb * per_block, (sb + 1) * per_block)
            k32 = jnp.concatenate([r[...] for r in kpages[pages]], axis=0)
            v32 = jnp.concatenate([r[...] for r in vpages[pages]], axis=0)
            blk = p * EVEN_BLOCKS + sb
            km_ref[pl.ds(blk, 1), :] = jnp.mean(k32[:, A_QK:], axis=0, keepdims=True)
            attend(k32.astype(BF16), v32.astype(BF16), None, blk)

    @pl.when(p == n_steps)
    def _():
        rows = 16 * n_new
        t = lax.broadcasted_iota(jnp.int32, (rows, NEW_PAD), 0) % n_new
        j = lax.broadcasted_iota(jnp.int32, (rows, NEW_PAD), 1)
        attend(knew_ref[0], vnew_ref[0], j <= t, n_blocks)

        lam = lam_ref[0]
        r = acca_ref[...] / la_ref[...]
        m_idx = lax.broadcasted_iota(jnp.int32, r.shape, 0) // n_new
        col = lax.broadcasted_iota(jnp.int32, r.shape, 1)
        coef = jnp.where(m_idx % 2 == 0, 1.0, -lam)
        oa = _fold_heads(jnp.where(col // A_VDIM == m_idx // 2, r * coef, 0.0), 8, n_new)
        for h in range(A_HEADS):
            cs = slice(h * A_VDIM, (h + 1) * A_VDIM)
            o_ref[0, :, cs] = (_rms(oa[:, cs], g_ref[...]) * out_scale).astype(BF16)

        nbp = km_ref.shape[0]
        gate = _dot_nt_f32(qbd[ra:, A_QK:].astype(F32), km_ref[...])
        blk = lax.broadcasted_iota(jnp.int32, gate.shape, 1)
        sel = _top3_select(jnp.where(blk < n_blocks, gate, -jnp.inf), n_blocks)
        chosen = [sel[:, n:n + 1] > 0.5 for n in range(n_blocks)]
        m_all = mb_ref[n_blocks]
        for n in range(n_blocks):
            m_all = jnp.where(chosen[n], jnp.maximum(m_all, mb_ref[n]), m_all)
        w_own = jnp.exp2(mb_ref[n_blocks] - m_all)
        l_all, acc = w_own * lb_ref[n_blocks], w_own * accb_ref[n_blocks]
        for n in range(n_blocks):
            w = jnp.where(chosen[n], jnp.exp2(mb_ref[n] - m_all), 0.0)
            l_all = l_all + w * lb_ref[n]
            acc = acc + w * accb_ref[n]
        r = acc / l_all
        h_idx = lax.broadcasted_iota(jnp.int32, r.shape, 0) // n_new
        col = lax.broadcasted_iota(jnp.int32, r.shape, 1)
        ob = _fold_heads(jnp.where(col // HEAD_DIM == h_idx, r, 0.0), 8, n_new)
        o_ref[0, :, A_V:] = ob.astype(BF16)


def _block_diag_rows(q, n_maps, width):
    s, t, c = q.shape
    keep = (jnp.arange(c)[None, :] // width) == jnp.arange(n_maps)[:, None]
    return jnp.where(keep[None, :, None, :], q[:, None, :, :], 0).reshape(s, n_maps * t, c).astype(q.dtype)


def _dec_even(page_table, lam, qbd, cache_k, cache_v, layer, knew, vnew, subln_g, out_scale):
    nseq, n_pages = page_table.shape
    n_new = qbd.shape[1] // 16
    n_blocks = n_pages * PAGE_SIZE // MOBA_BLOCK
    n_steps = n_pages // EVEN_PAGES
    width = cache_k.shape[-1]

    def page_spec(r):
        return pl.BlockSpec((None, None, PAGE_SIZE, width),
                            lambda s, p, pt, lm: (pt[s, jnp.minimum(p, n_steps - 1) * EVEN_PAGES + r], layer, 0, 0))

    seq_spec = lambda shape: pl.BlockSpec((1,) + shape, lambda s, p, pt, lm: (s, 0, 0))
    ra = 8 * n_new
    nbp = -(-(n_blocks + 1) // 8) * 8
    return pl.pallas_call(
        functools.partial(_dec_even_kernel, n_new=n_new, n_blocks=n_blocks, out_scale=out_scale),
        grid_spec=pltpu.PrefetchScalarGridSpec(
            num_scalar_prefetch=2,
            grid=(nseq, n_steps + 1),
            in_specs=[seq_spec((16 * n_new, width))]
            + [page_spec(r) for r in range(EVEN_PAGES)] + [page_spec(r) for r in range(EVEN_PAGES)]
            + [seq_spec((NEW_PAD, width)), seq_spec((NEW_PAD, width)),
               pl.BlockSpec((1, A_VDIM), lambda s, p, pt, lm: (0, 0))],
            out_specs=seq_spec((n_new, A_V + B_W)),
            scratch_shapes=[pltpu.VMEM((ra, 1), F32), pltpu.VMEM((ra, 1), F32), pltpu.VMEM((ra, A_V), F32),
                            pltpu.VMEM((n_blocks + 1, ra, 1), F32), pltpu.VMEM((n_blocks + 1, ra, 1), F32),
                            pltpu.VMEM((n_blocks + 1, ra, B_W), F32), pltpu.VMEM((nbp, B_W), F32)],
        ),
        out_shape=jax.ShapeDtypeStruct((nseq, n_new, A_V + B_W), BF16),
        compiler_params=_params(("parallel", "arbitrary")),
        name="dec_even",
    )(page_table, lam, qbd, *([cache_k] * EVEN_PAGES), *([cache_v] * EVEN_PAGES), knew, vnew, subln_g)


ODD_PAGES = 16
ODD_CHUNK = ODD_PAGES * PAGE_SIZE


def _dec_select_kernel(pt_ref, qi_ref, wi_ref, *refs, n_new, n_chunks, topk):
    pages = refs[:ODD_PAGES]
    kinew_ref, bias_ref, key_ref, tri_ref = refs[ODD_PAGES:]
    p = pl.program_id(1)
    qi = qi_ref[0]
    wi = wi_ref[0]

    def scores(kit16):
        lg = _dot(qi, kit16)
        return _fold_heads(wi * jnp.maximum(lg, 0.0), IDX_HEADS, n_new)

    @pl.when(p < n_chunks)
    def _():
        for r in range(ODD_PAGES):
            key_ref[p, :, r * PAGE_SIZE:(r + 1) * PAGE_SIZE] = _order_key(scores(pages[r][...].astype(BF16)))

    @pl.when(p == n_chunks)
    def _():
        sc = scores(kinew_ref[0])
        t = lax.broadcasted_iota(jnp.int32, sc.shape, 0)
        j = lax.broadcasted_iota(jnp.int32, sc.shape, 1)
        key_ref[n_chunks] = jnp.full((n_new, ODD_CHUNK), INT_MIN, jnp.int32)
        key_ref[n_chunks, :, 0:PAGE_SIZE] = jnp.where(j <= t, _order_key(sc), INT_MIN)
        qpos = n_chunks * ODD_CHUNK + lax.broadcasted_iota(jnp.int32, (n_new, 1), 0)
        kk = jnp.minimum(topk, qpos + 1).astype(F32)
        _topk_bias(key_ref, bias_ref.at[0], tri_ref, n_chunks + 1, kk, (qpos + 1).astype(F32), axis=1)


def _dec_select(page_table, qi, wi, cache_kit, layer, kinew_t):
    nseq, n_pages = page_table.shape
    n_new = qi.shape[1] // IDX_HEADS
    n_chunks = n_pages // ODD_PAGES
    topk = min(DSA_TOPK, (n_pages * PAGE_SIZE + n_new) // 4)

    def page_spec(r):
        return pl.BlockSpec((None, None, IDX_DIM, PAGE_SIZE),
                            lambda s, p, pt: (pt[s, jnp.minimum(p, n_chunks - 1) * ODD_PAGES + r], layer, 0, 0))

    return pl.pallas_call(
        functools.partial(_dec_select_kernel, n_new=n_new, n_chunks=n_chunks, topk=topk),
        grid_spec=pltpu.PrefetchScalarGridSpec(
            num_scalar_prefetch=1,
            grid=(nseq, n_chunks + 1),
            in_specs=[pl.BlockSpec((1, IDX_HEADS * n_new, IDX_DIM), lambda s, p, pt: (s, 0, 0)),
                      pl.BlockSpec((1, IDX_HEADS * n_new, 1), lambda s, p, pt: (s, 0, 0))]
            + [page_spec(r) for r in range(ODD_PAGES)]
            + [pl.BlockSpec((1, IDX_DIM, PAGE_SIZE), lambda s, p, pt: (s, 0, 0))],
            out_specs=pl.BlockSpec((1, n_chunks + 1, n_new, ODD_CHUNK), lambda s, p, pt: (s, 0, 0, 0)),
            scratch_shapes=[pltpu.VMEM((n_chunks + 1, n_new, ODD_CHUNK), jnp.int32),
                            pltpu.VMEM((ODD_CHUNK, ODD_CHUNK), BF16)],
        ),
        out_shape=jax.ShapeDtypeStruct((nseq, n_chunks + 1, n_new, ODD_CHUNK), F32),
        compiler_params=_params(("parallel", "arbitrary")),
        name="dec_select",
    )(page_table, qi, wi, *([cache_kit] * ODD_PAGES), kinew_t)


def _dec_odd_kernel(pt_ref, qbd_ref, bias_ref, *refs, n_new, n_chunks):
    kpages = refs[:ODD_PAGES]
    vpages = refs[ODD_PAGES:2 * ODD_PAGES]
    knew_ref, vnew_ref, o_ref, m_ref, l_ref, acc_ref = refs[2 * ODD_PAGES:]
    p = pl.program_id(1)
    qbd = qbd_ref[0]

    @pl.when(p == 0)
    def _():
        m_ref[...] = jnp.full_like(m_ref, NEG)
        l_ref[...] = jnp.zeros_like(l_ref)
        acc_ref[...] = jnp.zeros_like(acc_ref)

    def attend(k16, v16, bias):
        s = _dot_nt(qbd, k16) + jnp.tile(bias, (C_HEADS, 1))
        m_prev = m_ref[...]
        m_new = jnp.maximum(m_prev, jnp.max(s, axis=1, keepdims=True))
        alpha = jnp.exp2(m_prev - m_new)
        pr = jnp.exp2(s - m_new)
        l_ref[...] = alpha * l_ref[...] + jnp.sum(pr, axis=1, keepdims=True)
        acc_ref[...] = alpha * acc_ref[...] + _dot(pr.astype(BF16), v16)
        m_ref[...] = m_new

    @pl.when(p < n_chunks)
    def _():
        k16 = jnp.concatenate([r[...] for r in kpages], axis=0).astype(BF16)
        v16 = jnp.concatenate([r[...] for r in vpages], axis=0).astype(BF16)
        attend(k16, v16, bias_ref[0, 0])

    @pl.when(p == n_chunks)
    def _():
        attend(knew_ref[0], vnew_ref[0], bias_ref[0, 0, :, 0:NEW_PAD])
        r = acc_ref[...] / l_ref[...]
        group = C_HEADS // C_KV_HEADS
        for h in range(C_HEADS):
            g = h // group
            o_ref[0, :, h * HEAD_DIM:(h + 1) * HEAD_DIM] = (
                r[h * n_new:(h + 1) * n_new, g * HEAD_DIM:(g + 1) * HEAD_DIM].astype(BF16))


def _dec_odd(page_table, qbd, bias, cache_k, cache_v, layer, knew, vnew):
    nseq, n_pages = page_table.shape
    n_new = qbd.shape[1] // C_HEADS
    n_chunks = n_pages // ODD_PAGES

    def page_spec(r):
        return pl.BlockSpec((None, None, PAGE_SIZE, C_KV),
                            lambda s, p, pt: (pt[s, jnp.minimum(p, n_chunks - 1) * ODD_PAGES + r], layer, 0, 0))

    seq_spec = lambda shape: pl.BlockSpec((1,) + shape, lambda s, p, pt: (s, 0, 0))
    rows = C_HEADS * n_new
    return pl.pallas_call(
        functools.partial(_dec_odd_kernel, n_new=n_new, n_chunks=n_chunks),
        grid_spec=pltpu.PrefetchScalarGridSpec(
            num_scalar_prefetch=1,
            grid=(nseq, n_chunks + 1),
            in_specs=[seq_spec((rows, C_KV)),
                      pl.BlockSpec((1, 1, n_new, ODD_CHUNK), lambda s, p, pt: (s, p, 0, 0))]
            + [page_spec(r) for r in range(ODD_PAGES)] + [page_spec(r) for r in range(ODD_PAGES)]
            + [seq_spec((NEW_PAD, C_KV)), seq_spec((NEW_PAD, C_KV))],
            out_specs=seq_spec((n_new, C_Q)),
            scratch_shapes=[pltpu.VMEM((rows, 1), F32), pltpu.VMEM((rows, 1), F32), pltpu.VMEM((rows, C_KV), F32)],
        ),
        out_shape=jax.ShapeDtypeStruct((nseq, n_new, C_Q), BF16),
        compiler_params=_params(("parallel", "arbitrary")),
        name="dec_odd",
    )(page_table, qbd, bias, *([cache_k] * ODD_PAGES), *([cache_v] * ODD_PAGES), knew, vnew)


ROW_TILE = 1024
FFN_TF = 256
ATTN_TQ = 256
ATTN_TK = 512


def _pad_rows(a, rows):
    return jnp.pad(a, ((0, 0), (0, rows - a.shape[1]), (0, 0)))


def _gqa_rows(q):
    s, t, nh, d = q.shape
    group = nh // C_KV_HEADS
    keep = (jnp.arange(nh)[:, None] // group) == jnp.arange(C_KV_HEADS)[None, :]
    qh = jnp.swapaxes(q, 1, 2)
    out = jnp.where(keep[None, :, None, :, None], qh[:, :, :, None, :], 0)
    return out.reshape(s, nh * t, C_KV_HEADS * d).astype(q.dtype)


def kernel(x_prompt, x_sample, cache_k_even, cache_v_even, cache_k_odd, cache_v_odd, cache_kidx_odd, state_conv,
           page_table, g_mix, g_ffn, g_final, w_in_even, w_out_even, lam_even, subln_even, w_in_odd, w_out_odd,
           w_gate, w_up, conv_w, conv_b, w_down):
    bsz, seq, d = x_prompt.shape
    nseq, n_new, _ = x_sample.shape
    depth = g_mix.shape[0]
    ff = w_gate.shape[-1]
    past_len = page_table.shape[1] * PAGE_SIZE
    bf = lambda a: a.astype(BF16)

    cos_p, sin_p = _rope_tables(jnp.arange(seq, dtype=jnp.int32))
    cos_s, sin_s = _rope_tables(jnp.tile(past_len + jnp.arange(n_new, dtype=jnp.int32), nseq))
    ecols, ocols = _even_cols(), _odd_cols()
    cache_kit = jnp.swapaxes(cache_kidx_odd, 2, 3)
    qb0 = 2 * A_QK + A_V
    tm = min(ROW_TILE, seq)
    ms = nseq * n_new

    hp = x_prompt.reshape(bsz * seq, d)
    hs = x_sample.reshape(ms, d)
    ke_p, ve_p, ko_p, vo_p, kio_p, cs_p = [], [], [], [], [], []
    ke_s, ve_s, ko_s, vo_s, kio_s, cs_s = [], [], [], [], [], []
    for layer in range(depth):
        e = layer // 2
        g = g_mix[layer][None]
        if layer % 2 == 0:
            w16, wo = bf(w_in_even[e]), bf(w_out_even[e])
            lam_init = 0.8 - 0.6 * math.exp(-0.3 * layer)
            lp = lam_even[e].astype(F32)
            lam = (jnp.exp(jnp.sum(lp[0] * lp[1])) - jnp.exp(jnp.sum(lp[2] * lp[3])) + lam_init).reshape(1)
            sg = subln_even[e][None]
            out_scale = 1.0 - lam_init

            p32, p16 = _inproj(hp, g, w16, cos_p, sin_p, *ecols, tm=tm)
            p32, p16 = p32.reshape(bsz, seq, -1), p16.reshape(bsz, seq, -1)
            pair = jnp.arange(2 * A_HEADS) % 2
            qat = _heads_t(p16[..., :A_QK].reshape(bsz, seq, 2 * A_HEADS, HEAD_DIM), pair)
            tk = min(ATTN_TK, seq)
            vat = jnp.swapaxes(p16[..., 2 * A_QK:qb0].reshape(bsz, seq // tk, tk, A_V), 2, 3)
            oat = _diff_attn(lam, qat, p16, vat, sg.reshape(A_VDIM, 1), out_scale, tq=ATTN_TQ, tk=tk)
            kmean = _blockmean(p32, (qb0 + B_W) // B_W, B_W, nb_step=min(8, seq // MOBA_BLOCK))
            qbt = _heads_t(p16[..., qb0:qb0 + B_W].reshape(bsz, seq, B_HEADS, HEAD_DIM), pair)
            q32t = jnp.swapaxes(p32[..., qb0:qb0 + B_W], 1, 2)
            vbt = jnp.swapaxes(p16[..., qb0 + 2 * B_W:].reshape(bsz, seq // MOBA_BLOCK, MOBA_BLOCK, B_W), 2, 3)
            obt = _moba_attn(qbt, q32t, p16, vbt, kmean)
            o = jnp.swapaxes(jnp.concatenate([oat, obt], axis=1), 1, 2)
            hp = _outproj([o.reshape(bsz * seq, -1)], [wo], hp, tm=tm)
            ke_p.append(jnp.concatenate([p32[..., A_QK:2 * A_QK], p32[..., qb0 + B_W:qb0 + 2 * B_W]], -1))
            ve_p.append(jnp.concatenate([p32[..., 2 * A_QK:qb0], p32[..., qb0 + 2 * B_W:]], -1))

            s32, s16 = _inproj(hs, g, w16, cos_s, sin_s, *ecols, tm=ms)
            s32, s16 = s32.reshape(nseq, n_new, -1), s16.reshape(nseq, n_new, -1)
            qbd = _block_diag_rows(jnp.concatenate([s16[..., :A_QK], s16[..., qb0:qb0 + B_W]], -1), 16, HEAD_DIM)
            knew = jnp.concatenate([s16[..., A_QK:2 * A_QK], s16[..., qb0 + B_W:qb0 + 2 * B_W]], -1)
            vnew = jnp.concatenate([s16[..., 2 * A_QK:qb0], s16[..., qb0 + 2 * B_W:]], -1)
            o = _dec_even(page_table, lam, qbd, cache_k_even, cache_v_even, e,
                          _pad_rows(knew, NEW_PAD), _pad_rows(vnew, NEW_PAD), sg, out_scale)
            hs = _outproj([o.reshape(ms, -1)], [wo], hs, tm=ms)
            ke_s.append(jnp.concatenate([s32[..., A_QK:2 * A_QK], s32[..., qb0 + B_W:qb0 + 2 * B_W]], -1))
            ve_s.append(jnp.concatenate([s32[..., 2 * A_QK:qb0], s32[..., qb0 + 2 * B_W:]], -1))
        else:
            w16 = bf(jnp.pad(w_in_odd[e], ((0, 0), (0, ODD_NPAD - ODD_N))))
            wo = bf(w_out_odd[e])
            kc, vc, kic, wic = C_Q, C_Q + C_KV, ODD_KI_COL, ODD_KI_COL + IDX_DIM

            p32, p16 = _inproj(hp, g, w16, cos_p, sin_p, *ocols, tm=tm)
            p32, p16 = p32.reshape(bsz, seq, -1), p16.reshape(bsz, seq, -1)
            tc = min(ATTN_TK, seq)
            qt = _heads_t(p16[..., :C_Q].reshape(bsz, seq, C_HEADS, HEAD_DIM),
                          (jnp.arange(C_HEADS) // (C_HEADS // C_KV_HEADS)) % 2)
            qit = _heads_t(p16[..., vc + C_KV:kic].reshape(bsz, seq, IDX_HEADS, IDX_DIM), jnp.zeros(IDX_HEADS, jnp.int32))
            wit = jnp.swapaxes(p32[..., wic:wic + IDX_HEADS], 1, 2)
            vt4 = jnp.swapaxes(p16[..., vc:vc + C_KV].reshape(bsz, seq // tc, tc, C_KV), 2, 3)
            ot = _dsa_attn(qt, qit, wit, p16, vt4, tq=ATTN_TQ, tc=tc)
            hp = _outproj([jnp.swapaxes(ot, 1, 2).reshape(bsz * seq, -1)], [wo], hp, tm=tm)
            ko_p.append(p32[..., kc:kc + C_KV])
            vo_p.append(p32[..., vc:vc + C_KV])
            kio_p.append(p32[..., kic:kic + IDX_DIM])

            s32, s16 = _inproj(hs, g, w16, cos_s, sin_s, *ocols, tm=ms)
            s32, s16 = s32.reshape(nseq, n_new, -1), s16.reshape(nseq, n_new, -1)
            qi = s16[..., vc + C_KV:kic].reshape(nseq, n_new, IDX_HEADS, IDX_DIM)
            qi = jnp.swapaxes(qi, 1, 2).reshape(nseq, IDX_HEADS * n_new, IDX_DIM)
            wi = jnp.swapaxes(s32[..., wic:wic + IDX_HEADS], 1, 2).reshape(nseq, IDX_HEADS * n_new, 1)
            kinew_t = jnp.swapaxes(_pad_rows(s16[..., kic:kic + IDX_DIM], PAGE_SIZE), 1, 2)
            bias = _dec_select(page_table, qi, wi, cache_kit, e, kinew_t)
            qbd = _gqa_rows(s16[..., :C_Q].reshape(nseq, n_new, C_HEADS, HEAD_DIM))
            o = _dec_odd(page_table, qbd, bias, cache_k_odd, cache_v_odd, e,
                         _pad_rows(s16[..., kc:kc + C_KV], NEW_PAD), _pad_rows(s16[..., vc:vc + C_KV], NEW_PAD))
            hs = _outproj([o.reshape(ms, -1)], [wo], hs, tm=ms)
            ko_s.append(s32[..., kc:kc + C_KV])
            vo_s.append(s32[..., vc:vc + C_KV])
            kio_s.append(s32[..., kic:kic + IDX_DIM])

        last = layer == depth - 1
        ffn_w = (g_ffn[layer][None], bf(w_gate[layer]), bf(w_up[layer]), conv_w[layer], conv_b[layer][None],
                 bf(w_down[layer]), g_final[None])
        hp, tail = _ffn(hp, *ffn_w, seq_len=seq, tm=tm, tf=FFN_TF, final_norm=last)
        cs_p.append(tail.reshape(bsz, seq // tm, 8, ff)[:, -1, 8 - (CONV_W - 1):])
        hs, gout = _ffn_s(hs, state_conv[layer], *ffn_w, seq_len=n_new, tf=FFN_TF, final_norm=last)
        cs_s.append(gout.reshape(nseq, n_new, ff)[:, n_new - (CONV_W - 1):])

    def to_pages(rows):
        r = jnp.stack(rows, 1)
        b, nl, s, w = r.shape
        return r.reshape(b, nl, s // PAGE_SIZE, PAGE_SIZE, w).transpose(0, 2, 1, 3, 4)

    return (hp.reshape(bsz, seq, d), hs.reshape(nseq, n_new, d),
            to_pages(ke_p), to_pages(ve_p), to_pages(ko_p), to_pages(vo_p), to_pages(kio_p), jnp.stack(cs_p, 0),
            jnp.stack(ke_s, 1), jnp.stack(ve_s, 1), jnp.stack(ko_s, 1), jnp.stack(vo_s, 1), jnp.stack(kio_s, 1),
            jnp.stack(cs_s, 0))
```
